```python
import jax
import jax.numpy as jnp
from jax import lax
import numpy as np

D_MODEL = 2048
BATCH = 2
SEQ = 16384
DEPTH = 2

HEAD_DIM = 128
ROPE_DIM = HEAD_DIM // 4
ROPE_THETA = 500000.0
DILATED_GROUPS = ((128, 1), (512, 4), (2048, 16))
N_GROUPS = 3
ATTN_HEADS_PER_GROUP = D_MODEL // 512
ATTN_HEADS = N_GROUPS * ATTN_HEADS_PER_GROUP
ATTN_BLOCK = 128
GDN_HEADS = D_MODEL // 256
GDN_HEAD_DIM = 128
GDN_WIDTH = GDN_HEADS * GDN_HEAD_DIM
GDN_CHUNK = 64
SHORT_CONV = 4
CONV_CH = D_MODEL // 2
CONV_K = 31
D_FF = ((8 * D_MODEL + 3 * 256 - 1) // (3 * 256)) * 256
NORM_EPS = 1e-6
IN_SPLIT_SIZES = (ATTN_HEADS * HEAD_DIM, ATTN_HEADS * HEAD_DIM, ATTN_HEADS * HEAD_DIM, 3 * GDN_WIDTH, GDN_HEADS, GDN_HEADS, GDN_WIDTH, 2 * CONV_CH, 3 * D_MODEL)
N_IN = 3 * ATTN_HEADS * HEAD_DIM + 4 * GDN_WIDTH + 2 * GDN_HEADS + 2 * CONV_CH + 3 * D_MODEL

kernel_name = 'hybrid_dilated_attn_gdn_conformer_block'


def split_cols(t, sizes):
    idx = np.cumsum(np.array(sizes))[:-1].tolist()
    return jnp.split(t, idx, axis=-1)


def rms_norm(t, g):
    tf = t.astype(jnp.float32)
    y = tf * lax.rsqrt(jnp.mean(tf * tf, axis=-1, keepdims=True) + NORM_EPS)
    return (y * g.astype(jnp.float32)).astype(t.dtype)


def layer_norm(t, g, b):
    tf = t.astype(jnp.float32)
    mu = jnp.mean(tf, axis=-1, keepdims=True)
    var = jnp.mean(jnp.square(tf - mu), axis=-1, keepdims=True)
    y = (tf - mu) * lax.rsqrt(var + NORM_EPS)
    return (y * g.astype(jnp.float32) + b.astype(jnp.float32)).astype(t.dtype)


def l2_normalize(t):
    tf = t.astype(jnp.float32)
    return tf * lax.rsqrt(jnp.sum(tf * tf, axis=-1, keepdims=True) + NORM_EPS)


def adaln(c_act, w_mod, b_mod):
    mod = c_act @ w_mod + b_mod
    shift, scale, gate = jnp.split(mod, 3, axis=-1)
    return shift[:, None], scale[:, None], gate[:, None]


def causal_depthwise_conv(t, w):
    k, ch = w.shape
    return lax.conv_general_dilated(t, w[:, None, :].astype(t.dtype), window_strides=(1,), padding=[(k - 1, 0)], dimension_numbers=('NWC', 'WIO', 'NWC'), feature_group_count=ch)


def rope_tables(positions):
    inv = ROPE_THETA ** (-jnp.arange(0, ROPE_DIM, 2, dtype=jnp.float32) / ROPE_DIM)
    ang = positions.astype(jnp.float32)[..., None] * inv
    return jnp.cos(ang), jnp.sin(ang)


def apply_partial_rope(t, cos, sin):
    half = ROPE_DIM // 2
    x1 = t[..., :half].astype(jnp.float32)
    x2 = t[..., half:ROPE_DIM].astype(jnp.float32)
    rot = jnp.concatenate([x1 * cos - x2 * sin, x2 * cos + x1 * sin], axis=-1).astype(t.dtype)
    return jnp.concatenate([rot, t[..., ROPE_DIM:]], axis=-1)


def dilated_window_attention(q, k, v, dilation, steps):
    B, S, H, Dh = q.shape
    L = S // dilation
    nb = -(-L // ATTN_BLOCK)
    Lp = nb * ATTN_BLOCK

    def to_blocks(t):
        t = t.reshape(B, L, dilation, H, Dh).transpose(0, 2, 1, 3, 4)
        t = jnp.pad(t, ((0, 0), (0, 0), (0, Lp - L), (0, 0), (0, 0)))
        return t.reshape(B, dilation, nb, ATTN_BLOCK, H, Dh)

    def with_prev(t):
        prev = jnp.pad(t, ((0, 0), (0, 0), (1, 0), (0, 0), (0, 0), (0, 0)))[:, :, :-1]
        return jnp.concatenate([prev, t], axis=3)

    qb = to_blocks(q)
    kw = with_prev(to_blocks(k))
    vw = with_prev(to_blocks(v))
    s = jnp.einsum('brnqhd,brnkhd->brnhqk', qb, kw, preferred_element_type=jnp.float32) * (Dh ** -0.5)
    qi = jnp.arange(ATTN_BLOCK)[:, None]
    kj = jnp.arange(2 * ATTN_BLOCK)[None, :]
    dist = ATTN_BLOCK + qi - kj
    band = (dist >= 0) & (dist <= steps)
    key_idx = jnp.arange(nb)[:, None, None] * ATTN_BLOCK + kj[None] - ATTN_BLOCK
    valid = band[None] & (key_idx >= 0)
    s = jnp.where(valid[:, None], s, -jnp.inf)
    lse = jax.nn.logsumexp(s, axis=-1)
    p = jnp.exp(s - lse[..., None])
    o = jnp.einsum('brnhqk,brnkhd->brnqhd', p.astype(v.dtype), vw, preferred_element_type=jnp.float32)
    o = o.reshape(B, dilation, Lp, H, Dh)[:, :, :L].transpose(0, 2, 1, 3, 4).reshape(B, S, H, Dh)
    lse = lse.transpose(0, 1, 2, 4, 3).reshape(B, dilation, Lp, H)[:, :, :L].transpose(0, 2, 1, 3).reshape(B, S, H)
    return o, lse


def gated_delta_rule_chunked(q, k, v, g, beta):
    B, S, H, Dk = q.shape
    Dv = v.shape[-1]
    C = GDN_CHUNK
    N = S // C

    def chunks(t):
        t = t.reshape((B, N, C, H) + t.shape[3:])
        return jnp.moveaxis(t, 3, 1)

    q, k, v, g, beta = (chunks(t) for t in (q, k, v, g, beta))
    g = jnp.cumsum(g, axis=-1)
    causal = jnp.tril(jnp.ones((C, C), dtype=bool))
    strict = jnp.tril(jnp.ones((C, C), dtype=bool), -1)
    decay = jnp.exp(jnp.where(causal, g[..., :, None] - g[..., None, :], -jnp.inf))
    kk = jnp.einsum('bhnid,bhnjd->bhnij', k, k)
    a = jnp.where(strict, beta[..., :, None] * kk * decay, 0.0) + jnp.eye(C, dtype=q.dtype)
    rhs = jnp.concatenate([v * beta[..., None], k * (beta * jnp.exp(g))[..., None]], axis=-1)
    sol = lax.linalg.triangular_solve(a, rhs, left_side=True, lower=True, unit_diagonal=True)
    u, w = sol[..., :Dv], sol[..., Dv:]
    qk = jnp.einsum('bhnid,bhnjd->bhnij', q, k) * decay
    q_dec = q * jnp.exp(g)[..., None]
    g_last = g[..., -1]
    k_dec = k * jnp.exp(g_last[..., None] - g)[..., None]
    xs = tuple(jnp.moveaxis(t, 2, 0) for t in (q_dec, k_dec, u, w, qk, jnp.exp(g_last)))

    def step(state, inp):
        qc, kc, uc, wc, qkc, dc = inp
        v_new = uc - jnp.einsum('bhck,bhkv->bhcv', wc, state)
        o = jnp.einsum('bhck,bhkv->bhcv', qc, state) + jnp.einsum('bhij,bhjv->bhiv', qkc, v_new)
        state = state * dc[..., None, None] + jnp.einsum('bhck,bhcv->bhkv', kc, v_new)
        return state, o

    state0 = jnp.zeros((B, H, Dk, Dv), dtype=q.dtype)
    _, o = lax.scan(step, state0, xs)
    return o.transpose(1, 0, 3, 2, 4).reshape(B, S, H, Dv)


def hybrid_mixer(h, cos, sin, w_in, q_norm_g, k_norm_g, w_attn_o, gdn_conv_w, gdn_a_log, gdn_dt_bias, gdn_norm_g, w_gdn_o, conv_dw_w, conv_dw_b, conv_ln_g, conv_ln_b, w_conv_o, w_out):
    B, S, _ = h.shape
    proj = h @ w_in
    q_a, k_a, v_a, qkv_d, beta_d, alpha_d, z_d, u_c, gate_logits = split_cols(proj, IN_SPLIT_SIZES)

    heads = (B, S, N_GROUPS, ATTN_HEADS_PER_GROUP, HEAD_DIM)
    q_a = apply_partial_rope(rms_norm(q_a.reshape(heads), q_norm_g), cos, sin)
    k_a = apply_partial_rope(rms_norm(k_a.reshape(heads), k_norm_g), cos, sin)
    v_a = v_a.reshape(heads)
    outs, lses = [], []
    for gi, (window, dilation) in enumerate(DILATED_GROUPS):
        o, lse = dilated_window_attention(q_a[:, :, gi], k_a[:, :, gi], v_a[:, :, gi], dilation, window // dilation)
        outs.append(o)
        lses.append(lse)
    wts = jax.nn.softmax(jnp.stack(lses, axis=0), axis=0)
    o_a = jnp.einsum('gbsh,gbshd->bshd', wts, jnp.stack(outs, axis=0)).astype(h.dtype)
    y_a = o_a.reshape(B, S, ATTN_HEADS_PER_GROUP * HEAD_DIM) @ w_attn_o

    qkv = jax.nn.silu(causal_depthwise_conv(qkv_d, gdn_conv_w))
    q_d, k_d, v_d = jnp.split(qkv, 3, axis=-1)
    gh = (B, S, GDN_HEADS, GDN_HEAD_DIM)
    q_d = l2_normalize(q_d.reshape(gh)) * (GDN_HEAD_DIM ** -0.5)
    k_d = l2_normalize(k_d.reshape(gh))
    v_d = v_d.reshape(gh).astype(jnp.float32)
    beta = jax.nn.sigmoid(beta_d.astype(jnp.float32))
    log_decay = -jnp.exp(gdn_a_log.astype(jnp.float32)) * jax.nn.softplus(alpha_d.astype(jnp.float32) + gdn_dt_bias.astype(jnp.float32))
    o_d = gated_delta_rule_chunked(q_d, k_d, v_d, log_decay, beta)
    o_d = rms_norm(o_d, gdn_norm_g) * jax.nn.silu(z_d.reshape(gh).astype(jnp.float32))
    y_b = o_d.reshape(B, S, GDN_WIDTH).astype(h.dtype) @ w_gdn_o

    u_a, u_b = jnp.split(u_c, 2, axis=-1)
    u = u_a * jax.nn.sigmoid(u_b)
    u = causal_depthwise_conv(u, conv_dw_w) + conv_dw_b
    u = jax.nn.silu(layer_norm(u, conv_ln_g, conv_ln_b))
    y_c = u @ w_conv_o

    g_a, g_b, g_c = jnp.split(jax.nn.sigmoid(gate_logits), 3, axis=-1)
    return (g_a * y_a + g_b * y_b + g_c * y_c) @ w_out


def swiglu_ffn(h, w_gate_up, w_down):
    a, b = jnp.split(h @ w_gate_up, 2, axis=-1)
    return (jax.nn.silu(a) * b) @ w_down


def setup_inputs(seed: int = 0) -> dict:
    key = jax.random.key(seed)
    ks = list(jax.random.split(key, 32))
    f32 = jnp.float32
    L = DEPTH
    D = D_MODEL

    def nrm(k, shape, scale):
        return scale * jax.random.normal(k, shape, f32)

    def gain(k, shape):
        return 1.0 + 0.05 * jax.random.normal(k, shape, f32)

    x = jax.random.normal(ks[0], (BATCH, SEQ, D), f32)
    c = jax.random.normal(ks[1], (BATCH, D), f32)
    offset = jax.random.randint(ks[2], (BATCH, 1), 0, 4096, dtype=jnp.int32)
    positions = offset + jnp.arange(SEQ, dtype=jnp.int32)[None, :]
    attn_out_w = ATTN_HEADS_PER_GROUP * HEAD_DIM
    return {
        'x': x,
        'c': c,
        'positions': positions,
        'mix_mod_w': nrm(ks[3], (L, D, 3 * D), D ** -0.5),
        'mix_mod_b': nrm(ks[4], (L, 3 * D), 0.02),
        'mix_norm_g': gain(ks[5], (L, D)),
        'w_in': nrm(ks[6], (L, D, N_IN), D ** -0.5),
        'q_norm_g': gain(ks[7], (L, HEAD_DIM)),
        'k_norm_g': gain(ks[8], (L, HEAD_DIM)),
        'w_attn_o': nrm(ks[9], (L, attn_out_w, D), attn_out_w ** -0.5),
        'gdn_conv_w': nrm(ks[10], (L, SHORT_CONV, 3 * GDN_WIDTH), SHORT_CONV ** -0.5),
        'gdn_a_log': jnp.log(jax.random.uniform(ks[11], (L, GDN_HEADS), f32, 1.0, 16.0)),
        'gdn_dt_bias': nrm(ks[12], (L, GDN_HEADS), 0.1),
        'gdn_norm_g': gain(ks[13], (L, GDN_HEAD_DIM)),
        'w_gdn_o': nrm(ks[14], (L, GDN_WIDTH, D), GDN_WIDTH ** -0.5),
        'conv_dw_w': nrm(ks[15], (L, CONV_K, CONV_CH), CONV_K ** -0.5),
        'conv_dw_b': nrm(ks[16], (L, CONV_CH), 0.02),
        'conv_ln_g': gain(ks[17], (L, CONV_CH)),
        'conv_ln_b': nrm(ks[18], (L, CONV_CH), 0.02),
        'w_conv_o': nrm(ks[19], (L, CONV_CH, D), CONV_CH ** -0.5),
        'w_out': nrm(ks[20], (L, D, D), D ** -0.5),
        'ffn_mod_w': nrm(ks[21], (L, D, 3 * D), D ** -0.5),
        'ffn_mod_b': nrm(ks[22], (L, 3 * D), 0.02),
        'ffn_norm_g': gain(ks[23], (L, D)),
        'w_gate_up': nrm(ks[24], (L, D, 2 * D_FF), D ** -0.5),
        'w_down': nrm(ks[25], (L, D_FF, D), D_FF ** -0.5),
    }


def reference(x, c, positions, mix_mod_w, mix_mod_b, mix_norm_g, w_in, q_norm_g, k_norm_g, w_attn_o, gdn_conv_w, gdn_a_log, gdn_dt_bias, gdn_norm_g, w_gdn_o, conv_dw_w, conv_dw_b, conv_ln_g, conv_ln_b, w_conv_o, w_out, ffn_mod_w, ffn_mod_b, ffn_norm_g, w_gate_up, w_down):
    cos, sin = rope_tables(positions)
    cos = cos[:, :, None, None, :]
    sin = sin[:, :, None, None, :]
    c_act = jax.nn.silu(c)
    for l in range(DEPTH):
        shift, scale, gate = adaln(c_act, mix_mod_w[l], mix_mod_b[l])
        h = rms_norm(x, mix_norm_g[l]) * (1.0 + scale) + shift
        y = hybrid_mixer(h, cos, sin, w_in[l], q_norm_g[l], k_norm_g[l], w_attn_o[l], gdn_conv_w[l], gdn_a_log[l], gdn_dt_bias[l], gdn_norm_g[l], w_gdn_o[l], conv_dw_w[l], conv_dw_b[l], conv_ln_g[l], conv_ln_b[l], w_conv_o[l], w_out[l])
        x = x + (gate * y).astype(x.dtype)
        shift, scale, gate = adaln(c_act, ffn_mod_w[l], ffn_mod_b[l])
        h = rms_norm(x, ffn_norm_g[l]) * (1.0 + scale) + shift
        x = x + (gate * swiglu_ffn(h, w_gate_up[l], w_down[l])).astype(x.dtype)
    return x
```

```python
import functools

import jax
import jax.numpy as jnp
from jax import lax
from jax.experimental import pallas as pl
from jax.experimental.pallas import tpu as pltpu

F32 = jnp.float32
BF16 = jnp.bfloat16

D_MODEL = 2048
HEAD_DIM = 128
ROPE_DIM = HEAD_DIM // 4
ROPE_HALF = ROPE_DIM // 2
ROPE_THETA = 500000.0
DILATIONS = (1, 4, 16)
ATTN_BLOCK = 128
N_GROUPS = 3
GROUP_WIDTH = 4 * HEAD_DIM
ATTN_WIDTH = N_GROUPS * GROUP_WIDTH
GDN_HEADS = 8
GDN_WIDTH = GDN_HEADS * HEAD_DIM
GDN_CHUNK = 64
GDN_TILE = 256
SHORT_CONV = 4
CONV_CH = D_MODEL // 2
CONV_K = 31
D_FF = 5632
NORM_EPS = 1e-6
NEG_BIG = -1e30

VMEM_LIMIT_V7X = 56 * 1024 * 1024


def _params(sem, vmem=VMEM_LIMIT_V7X):
    return pltpu.CompilerParams(dimension_semantics=sem, vmem_limit_bytes=vmem)


def _silu(t):
    return t * jax.nn.sigmoid(t)


def _dot(a, b):
    return jnp.dot(a, b, preferred_element_type=F32)


def _dot_nt(a, b):
    return lax.dot_general(a, b, (((1,), (1,)), ((), ())), preferred_element_type=F32)


def _dot_tn(a, b):
    return lax.dot_general(a, b, (((0,), (0,)), ((), ())), preferred_element_type=F32)


def _split3(t):
    hi = t.astype(BF16)
    r = t - hi.astype(F32)
    mid = r.astype(BF16)
    lo = (r - mid.astype(F32)).astype(BF16)
    return hi, mid, lo


def _dot_f32(a, b):
    a0, a1, a2 = _split3(a)
    b0, b1, b2 = _split3(b)
    return (_dot(a0, b0) + (_dot(a0, b1) + _dot(a1, b0))
            + (_dot(a0, b2) + _dot(a1, b1) + _dot(a2, b0)))


def _mod_kernel(c_ref, w_ref, b_ref, o_ref):
    c = c_ref[...]
    o_ref[0] = _dot(_silu(c).astype(BF16), w_ref[0].astype(BF16)) + b_ref[0]


def _adaln_mods(c_pad, w, b):
    n_layers, d, n = w.shape
    tn = 1024
    return pl.pallas_call(
        _mod_kernel,
        grid=(n_layers, n // tn),
        in_specs=[
            pl.BlockSpec((8, d), lambda l, j: (0, 0)),
            pl.BlockSpec((1, d, tn), lambda l, j: (l, 0, j)),
            pl.BlockSpec((1, 1, tn), lambda l, j: (l, 0, j)),
        ],
        out_specs=pl.BlockSpec((1, 8, tn), lambda l, j: (l, 0, j)),
        out_shape=jax.ShapeDtypeStruct((n_layers, 8, n), F32),
        compiler_params=_params(("parallel", "parallel")),
        name="adaln_mods",
    )(c_pad, w, b.reshape(n_layers, 1, n))


def _norm_kernel(x_ref, g_ref, sc_ref, sh_ref, o_ref):
    x = x_ref[0]
    ms = jnp.mean(x * x, axis=-1, keepdims=True)
    y = x * lax.rsqrt(ms + NORM_EPS) * g_ref[...]
    o_ref[0] = (y * (1.0 + sc_ref[0]) + sh_ref[0]).astype(BF16)


def _mod_norm(x, g, scale, shift):
    bsz, s, d = x.shape
    tm = 512
    vec = pl.BlockSpec((1, 1, d), lambda b, i: (b, 0, 0))
    return pl.pallas_call(
        _norm_kernel,
        grid=(bsz, s // tm),
        in_specs=[
            pl.BlockSpec((1, tm, d), lambda b, i: (b, i, 0)),
            pl.BlockSpec((1, d), lambda b, i: (0, 0)),
            vec, vec,
        ],
        out_specs=pl.BlockSpec((1, tm, d), lambda b, i: (b, i, 0)),
        out_shape=jax.ShapeDtypeStruct((bsz, s, d), BF16),
        compiler_params=_params(("parallel", "parallel")),
        name="mod_norm",
    )(x, g, scale, shift)


def _rope_kernel(pos_ref, inv_ref, sgn_ref, cos_ref, sin_ref):
    ang = pos_ref[0] * inv_ref[...]
    cos_ref[0] = jnp.cos(ang)
    sin_ref[0] = jnp.sin(ang) * sgn_ref[...]


def _rope_tables(positions):
    bsz, s = positions.shape
    tm = 1024
    inv = ROPE_THETA ** (-jnp.arange(0, ROPE_DIM, 2, dtype=F32) / ROPE_DIM)
    zeros = jnp.zeros((HEAD_DIM - ROPE_DIM,), F32)
    inv_full = jnp.concatenate([inv, inv, zeros]).reshape(1, HEAD_DIM)
    sgn = jnp.concatenate([-jnp.ones((ROPE_HALF,), F32), jnp.ones((ROPE_HALF,), F32), zeros])
    pos = positions.astype(F32).reshape(bsz, s, 1)
    out = jax.ShapeDtypeStruct((bsz, s, HEAD_DIM), F32)
    row = pl.BlockSpec((1, HEAD_DIM), lambda b, i: (0, 0))
    blk = pl.BlockSpec((1, tm, HEAD_DIM), lambda b, i: (b, i, 0))
    return pl.pallas_call(
        _rope_kernel,
        grid=(bsz, s // tm),
        in_specs=[pl.BlockSpec((1, tm, 1), lambda b, i: (b, i, 0)), row, row],
        out_specs=[blk, blk],
        out_shape=[out, out],
        compiler_params=_params(("parallel", "parallel")),
        name="rope_tables",
    )(pos, inv_full, sgn.reshape(1, HEAD_DIM))


def _qkv_kernel(a_ref, w_ref, g_ref, cos_ref, sin_ref, o_ref):
    j = pl.program_id(2)
    acc = _dot(a_ref[0], w_ref[...])

    @pl.when(j < 2 * N_GROUPS)
    def _():
        cosf = cos_ref[0]
        sinf = sin_ref[0]
        lane = lax.broadcasted_iota(jnp.int32, (1, HEAD_DIM), 1)
        for hh in range(GROUP_WIDTH // HEAD_DIM):
            t = acc[:, hh * HEAD_DIM:(hh + 1) * HEAD_DIM]
            ms = jnp.mean(t * t, axis=-1, keepdims=True)
            t = t * lax.rsqrt(ms + NORM_EPS) * g_ref[0]
            partner = jnp.where(lane < ROPE_HALF,
                                pltpu.roll(t, HEAD_DIM - ROPE_HALF, 1),
                                pltpu.roll(t, ROPE_HALF, 1))
            t = t * cosf + partner * sinf
            o_ref[0, 0, :, hh * HEAD_DIM:(hh + 1) * HEAD_DIM] = t.astype(BF16)

    @pl.when(j >= 2 * N_GROUPS)
    def _():
        o_ref[0, 0] = acc.astype(BF16)


def _proj_qkv(h, w_qkv, gains, cosf, sinf):
    bsz, s, k = h.shape
    tm, tn = 1024, GROUP_WIDTH
    nt = w_qkv.shape[1] // tn
    return pl.pallas_call(
        _qkv_kernel,
        grid=(bsz, s // tm, nt),
        in_specs=[
            pl.BlockSpec((1, tm, k), lambda b, i, j: (b, i, 0)),
            pl.BlockSpec((k, tn), lambda b, i, j: (0, j)),
            pl.BlockSpec((1, 1, HEAD_DIM), lambda b, i, j: (j, 0, 0)),
            pl.BlockSpec((1, tm, HEAD_DIM), lambda b, i, j: (b, i, 0)),
            pl.BlockSpec((1, tm, HEAD_DIM), lambda b, i, j: (b, i, 0)),
        ],
        out_specs=pl.BlockSpec((1, 1, tm, tn), lambda b, i, j: (j, b, i, 0)),
        out_shape=jax.ShapeDtypeStruct((nt, bsz, s, tn), BF16),
        compiler_params=_params(("parallel", "parallel", "arbitrary")),
        name="proj_qkv",
    )(h, w_qkv, gains, cosf, sinf)


def _mm_kernel(a_ref, *refs, mode):
    o_ref = refs[-1]
    a = a_ref[0]
    if mode == "plain":
        o_ref[0] = _dot(a, refs[0][...]).astype(o_ref.dtype)
    elif mode == "silu":
        o_ref[0] = _silu(_dot(a, refs[0][...])).astype(o_ref.dtype)
    elif mode == "glu":
        o_ref[0] = (_dot(a, refs[0][...]) * jax.nn.sigmoid(_dot(a, refs[1][...]))).astype(o_ref.dtype)
    elif mode == "swiglu":
        o_ref[0] = (_silu(_dot(a, refs[0][...])) * _dot(a, refs[1][...])).astype(o_ref.dtype)
    elif mode == "residual":
        x_ref, gate_ref = refs[1], refs[2]
        o_ref[0] = x_ref[0] + gate_ref[0] * _dot(a, refs[0][...])
    else:
        raise ValueError(mode)


def _mm(a, w, *, mode, out_dtype, tn, tm=1024, col0=0, col1=0, n_out=None, x=None, gate=None):
    bsz, s, k = a.shape
    n_out = w.shape[1] if n_out is None else n_out
    assert col0 % tn == 0 and col1 % tn == 0 and n_out % tn == 0 and s % tm == 0
    j0, j1 = col0 // tn, col1 // tn
    in_specs = [pl.BlockSpec((1, tm, k), lambda b, i, j: (b, i, 0)),
                pl.BlockSpec((k, tn), lambda b, i, j: (0, j0 + j))]
    args = [a, w]
    if mode in ("glu", "swiglu"):
        in_specs.append(pl.BlockSpec((k, tn), lambda b, i, j: (0, j1 + j)))
        args.append(w)
    if mode == "residual":
        in_specs += [pl.BlockSpec((1, tm, tn), lambda b, i, j: (b, i, j)),
                     pl.BlockSpec((1, 1, tn), lambda b, i, j: (b, 0, j))]
        args += [x, gate]
    return pl.pallas_call(
        functools.partial(_mm_kernel, mode=mode),
        grid=(bsz, s // tm, n_out // tn),
        in_specs=in_specs,
        out_specs=pl.BlockSpec((1, tm, tn), lambda b, i, j: (b, i, j)),
        out_shape=jax.ShapeDtypeStruct((bsz, s, n_out), out_dtype),
        compiler_params=_params(("parallel", "parallel", "arbitrary")),
        name="mm_" + mode,
    )(*args)


def _attn_kernel(q_ref, kc_ref, kp_ref, vc_ref, vp_ref, o_ref, l_ref, *, nq):
    n = pl.program_id(2)
    qi = lax.broadcasted_iota(jnp.int32, (ATTN_BLOCK, ATTN_BLOCK), 0)
    kj = lax.broadcasted_iota(jnp.int32, (ATTN_BLOCK, ATTN_BLOCK), 1)
    cur_ok = kj <= qi
    prev_band = kj >= qi
    for t in range(nq):
        rows = slice(t * ATTN_BLOCK, (t + 1) * ATTN_BLOCK)
        prows = slice((t - 1) * ATTN_BLOCK, t * ATTN_BLOCK)
        prev_ok = jnp.logical_and(prev_band, n > 0) if t == 0 else prev_band
        for hh in range(GROUP_WIDTH // HEAD_DIM):
            cols = slice(hh * HEAD_DIM, (hh + 1) * HEAD_DIM)
            q = q_ref[0, 0, rows, cols]
            kc = kc_ref[0, 0, rows, cols]
            vc = vc_ref[0, 0, rows, cols]
            if t == 0:
                kp = kp_ref[0, 0, :, cols]
                vp = vp_ref[0, 0, :, cols]
            else:
                kp = kc_ref[0, 0, prows, cols]
                vp = vc_ref[0, 0, prows, cols]
            s_c = jnp.where(cur_ok, _dot_nt(q, kc), NEG_BIG)
            s_p = jnp.where(prev_ok, _dot_nt(q, kp), NEG_BIG)
            m = jnp.maximum(jnp.max(s_c, axis=-1, keepdims=True), jnp.max(s_p, axis=-1, keepdims=True))
            p_c = jnp.exp(s_c - m)
            p_p = jnp.exp(s_p - m)
            den = jnp.sum(p_c, axis=-1, keepdims=True) + jnp.sum(p_p, axis=-1, keepdims=True)
            o = _dot(p_c.astype(BF16), vc) + _dot(p_p.astype(BF16), vp)
            o_ref[0, rows, cols] = o / den
            l_ref[0, rows, cols] = jnp.broadcast_to(m + jnp.log(den), (ATTN_BLOCK, HEAD_DIM))


def _attn_group(qkv9, g, nq=1):
    _, bsz, s, w = qkv9.shape
    r = DILATIONS[g]
    l = s // r
    nb = l // (ATTN_BLOCK * nq)
    view = qkv9.reshape(9, bsz, l, r * w)
    tq = ATTN_BLOCK * nq

    def cur(kind):
        return pl.BlockSpec((1, 1, tq, w), lambda b, rr, n: (kind * N_GROUPS + g, b, n, rr))

    def prev(kind):
        return pl.BlockSpec((1, 1, ATTN_BLOCK, w),
                            lambda b, rr, n: (kind * N_GROUPS + g, b, jnp.maximum(n * nq - 1, 0), rr))

    out = jax.ShapeDtypeStruct((bsz, l, r * w), F32)
    oblk = pl.BlockSpec((1, tq, w), lambda b, rr, n: (b, n, rr))
    o, lse = pl.pallas_call(
        functools.partial(_attn_kernel, nq=nq),
        grid=(bsz, r, nb),
        in_specs=[cur(0), cur(1), prev(1), cur(2), prev(2)],
        out_specs=[oblk, oblk],
        out_shape=[out, out],
        compiler_params=_params(("parallel", "parallel", "arbitrary")),
        name=f"attn_g{g}",
    )(view, view, view, view, view)
    return o.reshape(bsz, s, w), lse.reshape(bsz, s, w)


def _attn_merge_kernel(o0, o1, o2, l0, l1, l2, out_ref):
    a0, a1, a2 = l0[0], l1[0], l2[0]
    m = jnp.maximum(jnp.maximum(a0, a1), a2)
    e0, e1, e2 = jnp.exp(a0 - m), jnp.exp(a1 - m), jnp.exp(a2 - m)
    num = e0 * o0[0] + e1 * o1[0] + e2 * o2[0]
    out_ref[0] = (num / (e0 + e1 + e2)).astype(BF16)


def _attn_merge(outs, lses):
    bsz, s, w = outs[0].shape
    tm = 1024
    blk = pl.BlockSpec((1, tm, w), lambda b, i: (b, i, 0))
    return pl.pallas_call(
        _attn_merge_kernel,
        grid=(bsz, s // tm),
        in_specs=[blk] * 6,
        out_specs=blk,
        out_shape=jax.ShapeDtypeStruct((bsz, s, w), BF16),
        compiler_params=_params(("parallel", "parallel")),
        name="attn_merge",
    )(*outs, *lses)


def _gdn_prep_kernel(x_ref, xp_ref, w_ref, q_ref, k_ref, v_ref, scr, *, tm):
    i = pl.program_id(1)
    pad = 8
    scr[0:pad, :] = jnp.where(i > 0, xp_ref[0], 0.0)
    scr[pad:pad + tm, :] = x_ref[0]
    n_blocks = 3 * GDN_HEADS
    for c in range(n_blocks):
        cols = slice(c * HEAD_DIM, (c + 1) * HEAD_DIM)
        acc = None
        for kk in range(SHORT_CONV):
            term = w_ref[kk:kk + 1, cols] * scr[pl.ds(pad - SHORT_CONV + 1 + kk, tm), cols]
            acc = term if acc is None else acc + term
        y = _silu(acc)
        hcols = slice((c % GDN_HEADS) * HEAD_DIM, (c % GDN_HEADS + 1) * HEAD_DIM)
        if c < 2 * GDN_HEADS:
            y = y * lax.rsqrt(jnp.sum(y * y, axis=-1, keepdims=True) + NORM_EPS)
            if c < GDN_HEADS:
                q_ref[0, :, hcols] = y * (HEAD_DIM ** -0.5)
            else:
                k_ref[0, :, hcols] = y
        else:
            v_ref[0, :, hcols] = y


def _gdn_prep(qkv_d, conv_w):
    bsz, s, n = qkv_d.shape
    tm = 256
    out = jax.ShapeDtypeStruct((bsz, s, GDN_WIDTH), F32)
    oblk = pl.BlockSpec((1, tm, GDN_WIDTH), lambda b, i: (b, i, 0))
    return pl.pallas_call(
        functools.partial(_gdn_prep_kernel, tm=tm),
        grid=(bsz, s // tm),
        in_specs=[
            pl.BlockSpec((1, tm, n), lambda b, i: (b, i, 0)),
            pl.BlockSpec((1, 8, n), lambda b, i: (b, jnp.maximum(i * (tm // 8) - 1, 0), 0)),
            pl.BlockSpec((SHORT_CONV, n), lambda b, i: (0, 0)),
        ],
        out_specs=[oblk, oblk, oblk],
        out_shape=[out, out, out],
        scratch_shapes=[pltpu.VMEM((tm + 8, n), F32)],
        compiler_params=_params(("parallel", "arbitrary")),
        name="gdn_prep",
    )(qkv_d, qkv_d, conv_w)


def _conformer_kernel(u_ref, up_ref, w_ref, b_ref, g_ref, beta_ref, o_ref, scr, y_scr, *, tm):
    i = pl.program_id(1)
    pad = 32
    scr[0:pad, :] = jnp.where(i > 0, up_ref[0], 0.0)
    scr[pad:pad + tm, :] = u_ref[0]
    for c in range(CONV_CH // HEAD_DIM):
        cols = slice(c * HEAD_DIM, (c + 1) * HEAD_DIM)
        acc = None
        for kk in range(CONV_K):
            term = w_ref[kk:kk + 1, cols] * scr[pl.ds(pad - CONV_K + 1 + kk, tm), cols]
            acc = term if acc is None else acc + term
        y_scr[:, cols] = acc + b_ref[:, cols]
    y = y_scr[...]
    mu = jnp.mean(y, axis=-1, keepdims=True)
    yc = y - mu
    var = jnp.mean(yc * yc, axis=-1, keepdims=True)
    t = yc * lax.rsqrt(var + NORM_EPS) * g_ref[...] + beta_ref[...]
    o_ref[0] = _silu(t).astype(BF16)


def _conformer(u, w, b, g, beta):
    bsz, s, n = u.shape
    tm = 256
    row = pl.BlockSpec((1, n), lambda bb, i: (0, 0))
    return pl.pallas_call(
        functools.partial(_conformer_kernel, tm=tm),
        grid=(bsz, s // tm),
        in_specs=[
            pl.BlockSpec((1, tm, n), lambda bb, i: (bb, i, 0)),
            pl.BlockSpec((1, 32, n), lambda bb, i: (bb, jnp.maximum(i * (tm // 32) - 1, 0), 0)),
            pl.BlockSpec((32, n), lambda bb, i: (0, 0)),
            row, row, row,
        ],
        out_specs=pl.BlockSpec((1, tm, n), lambda bb, i: (bb, i, 0)),
        out_shape=jax.ShapeDtypeStruct((bsz, s, n), BF16),
        scratch_shapes=[pltpu.VMEM((tm + 32, n), F32), pltpu.VMEM((tm, n), F32)],
        compiler_params=_params(("parallel", "arbitrary")),
        name="conformer_conv",
    )(u, u, w, b, g, beta)


def _gdn_scalar_kernel(ba_ref, alog_ref, dtb_ref, o_ref, *, ts):
    x = ba_ref[0]
    beta = jax.nn.sigmoid(x[0:GDN_HEADS])
    g = -jnp.exp(alog_ref[...]) * jax.nn.softplus(x[GDN_HEADS:2 * GDN_HEADS] + dtb_ref[...])
    lane = lax.broadcasted_iota(jnp.int32, (GDN_HEADS, ts), 1) % GDN_CHUNK
    fwd, bwd = g, g
    sh = 1
    while sh < GDN_CHUNK:
        fwd = fwd + jnp.where(lane >= sh, pltpu.roll(fwd, sh, 1), 0.0)
        bwd = bwd + jnp.where(lane < GDN_CHUNK - sh, pltpu.roll(bwd, ts - sh, 1), 0.0)
        sh *= 2
    rest = bwd - g
    eg = jnp.exp(fwd)
    o_ref[0, 0 * GDN_HEADS:1 * GDN_HEADS] = beta
    o_ref[0, 1 * GDN_HEADS:2 * GDN_HEADS] = eg
    o_ref[0, 2 * GDN_HEADS:3 * GDN_HEADS] = beta * eg
    o_ref[0, 3 * GDN_HEADS:4 * GDN_HEADS] = jnp.exp(rest)
    o_ref[0, 4 * GDN_HEADS:5 * GDN_HEADS] = fwd
    o_ref[0, 5 * GDN_HEADS:6 * GDN_HEADS] = jnp.exp(fwd + rest)


def _gdn_scalars(ba_t, a_log, dt_bias):
    bsz, _, s = ba_t.shape
    ts = 2048
    col = pl.BlockSpec((GDN_HEADS, 1), lambda b, i: (0, 0))
    return pl.pallas_call(
        functools.partial(_gdn_scalar_kernel, ts=ts),
        grid=(bsz, s // ts),
        in_specs=[pl.BlockSpec((1, 2 * GDN_HEADS, ts), lambda b, i: (b, 0, i)), col, col],
        out_specs=pl.BlockSpec((1, 6 * GDN_HEADS, ts), lambda b, i: (b, 0, i)),
        out_shape=jax.ShapeDtypeStruct((bsz, 6 * GDN_HEADS, s), F32),
        compiler_params=_params(("parallel", "parallel")),
        name="gdn_scalars",
    )(ba_t, a_log.reshape(GDN_HEADS, 1), dt_bias.reshape(GDN_HEADS, 1))


def _gdn_kernel(q_ref, k_ref, v_ref, z_ref, col_ref, grow_ref, dc_ref, ng_ref, o_ref, state):
    i = pl.program_id(2)

    @pl.when(i == 0)
    def _():
        state[...] = jnp.zeros_like(state)

    ts = GDN_TILE
    q = q_ref[0]
    k = k_ref[0]
    v = v_ref[0]
    col = col_ref[0, 0]
    beta, gam, bgam, kdec, gcol = (col[:, c:c + 1] for c in range(5))
    grow = grow_ref[0, 0]

    ri = lax.broadcasted_iota(jnp.int32, (ts, ts), 0)
    ci = lax.broadcasted_iota(jnp.int32, (ts, ts), 1)
    same = (ri // GDN_CHUNK) == (ci // GDN_CHUNK)
    causal = jnp.logical_and(same, ri >= ci)
    strict = jnp.logical_and(same, ri > ci)

    kb = k.astype(BF16)
    decay = jnp.exp(jnp.where(causal, gcol - grow, NEG_BIG))
    a = jnp.where(strict, beta * _dot_nt(kb, kb) * decay, 0.0)
    qk = _dot_nt(q.astype(BF16), kb) * decay

    rhs = jnp.concatenate([v * beta, k * bgam], axis=1)
    sol = rhs
    powers = [a]
    for _ in range(5):
        powers.append(_dot_f32(powers[-1], powers[-1]))
    for p in reversed(powers[1:]):
        sol = sol + _dot_f32(p, sol)
    sol = sol - _dot_f32(a, sol)
    u = sol[:, :HEAD_DIM]
    w = sol[:, HEAD_DIM:]

    qd = (q * gam).astype(BF16)
    kd = (k * kdec).astype(BF16)
    wb = w.astype(BF16)
    qkb = qk.astype(BF16)
    st = state[...]
    outs = []
    for c in range(ts // GDN_CHUNK):
        rows = slice(c * GDN_CHUNK, (c + 1) * GDN_CHUNK)
        sb = st.astype(BF16)
        v_new = u[rows] - _dot(wb[rows], sb)
        vb = v_new.astype(BF16)
        outs.append(_dot(qd[rows], sb) + _dot(qkb[rows, rows], vb))
        st = st * dc_ref[0, 0, 0, c:c + 1, :] + _dot_tn(kd[rows], vb)
    state[...] = st
    o = jnp.concatenate(outs, axis=0)
    o = o * lax.rsqrt(jnp.mean(o * o, axis=-1, keepdims=True) + NORM_EPS) * ng_ref[...]
    o_ref[0] = (o * z_ref[0].astype(F32)).astype(BF16)


def _gdn(q, k, v, z, cols, grow, dcb, norm_g):
    bsz, s, _ = q.shape
    ts = GDN_TILE
    blk = pl.BlockSpec((1, ts, HEAD_DIM), lambda b, h, i: (b, i, h))
    return pl.pallas_call(
        _gdn_kernel,
        grid=(bsz, GDN_HEADS, s // ts),
        in_specs=[
            blk, blk, blk, blk,
            pl.BlockSpec((1, 1, ts, 8), lambda b, h, i: (b, h, i, 0)),
            pl.BlockSpec((1, 1, 1, ts), lambda b, h, i: (b, h, 0, i)),
            pl.BlockSpec((1, 1, 1, ts // GDN_CHUNK, HEAD_DIM), lambda b, h, i: (b, h, i, 0, 0)),
            pl.BlockSpec((1, HEAD_DIM), lambda b, h, i: (0, 0)),
        ],
        out_specs=blk,
        out_shape=jax.ShapeDtypeStruct((bsz, s, GDN_WIDTH), BF16),
        scratch_shapes=[pltpu.VMEM((HEAD_DIM, HEAD_DIM), F32)],
        compiler_params=_params(("parallel", "parallel", "arbitrary")),
        name="gdn_delta",
    )(q, k, v, z, cols, grow, dcb, norm_g)


def _merge_kernel(oa_ref, od_ref, oc_ref, h_ref, wa_ref, wd_ref, wc_ref, ga_ref, gb_ref, gc_ref, o_ref):
    h = h_ref[0]
    acc = jax.nn.sigmoid(_dot(h, ga_ref[...])) * _dot(oa_ref[0], wa_ref[...])
    acc = acc + jax.nn.sigmoid(_dot(h, gb_ref[...])) * _dot(od_ref[0], wd_ref[...])
    acc = acc + jax.nn.sigmoid(_dot(h, gc_ref[...])) * _dot(oc_ref[0], wc_ref[...])
    o_ref[0] = acc.astype(BF16)


def _merge(o_a, o_d, o_c, h, w_a, w_d, w_c, w_in, gate_col0):
    bsz, s, d = h.shape
    tm, tn = 1024, 256
    j0 = gate_col0 // tn
    nd = d // tn

    def rows(width):
        return pl.BlockSpec((1, tm, width), lambda b, i, j: (b, i, 0))

    def wcols(kdim, off=0):
        return pl.BlockSpec((kdim, tn), lambda b, i, j: (0, off + j))

    return pl.pallas_call(
        _merge_kernel,
        grid=(bsz, s // tm, nd),
        in_specs=[rows(o_a.shape[2]), rows(o_d.shape[2]), rows(o_c.shape[2]), rows(d),
                  wcols(w_a.shape[0]), wcols(w_d.shape[0]), wcols(w_c.shape[0]),
                  wcols(d, j0), wcols(d, j0 + nd), wcols(d, j0 + 2 * nd)],
        out_specs=pl.BlockSpec((1, tm, tn), lambda b, i, j: (b, i, j)),
        out_shape=jax.ShapeDtypeStruct((bsz, s, d), BF16),
        compiler_params=_params(("parallel", "parallel", "arbitrary")),
        name="branch_merge",
    )(o_a, o_d, o_c, h, w_a, w_d, w_c, w_in, w_in, w_in)


_C_QKV = 0
_C_GDN = _C_QKV + 3 * ATTN_WIDTH
_C_Z = _C_GDN + 3 * GDN_WIDTH
_C_U = _C_Z + GDN_WIDTH
_C_GATE = _C_U + 2 * CONV_CH
_C_BA = _C_GATE + 3 * D_MODEL
_N_IN_PAD = _C_BA + 512


def _rearrange_w_in(w):
    o_ba = 3 * ATTN_WIDTH + 3 * GDN_WIDTH
    ba = w[:, o_ba:o_ba + 2 * GDN_HEADS]
    pad = jnp.zeros((w.shape[0], 512 - 2 * GDN_HEADS), w.dtype)
    return jnp.concatenate([w[:, :o_ba], w[:, o_ba + 2 * GDN_HEADS:], ba, pad], axis=1).astype(BF16)


def _mixer(h, l, cosf, sinf, w_in, q_norm_g, k_norm_g, w_attn_o, gdn_conv_w, gdn_a_log, gdn_dt_bias,
           gdn_norm_g, w_gdn_o, conv_dw_w, conv_dw_b, conv_ln_g, conv_ln_b, w_conv_o):
    bsz, s, _ = h.shape
    w = _rearrange_w_in(w_in[l])

    gains = jnp.concatenate([
        jnp.broadcast_to(q_norm_g[l] * (HEAD_DIM ** -0.5), (N_GROUPS, HEAD_DIM)),
        jnp.broadcast_to(k_norm_g[l], (N_GROUPS, HEAD_DIM)),
        jnp.ones((N_GROUPS, HEAD_DIM), F32)]).reshape(3 * N_GROUPS, 1, HEAD_DIM)
    qkv9 = _proj_qkv(h, w[:, :_C_GDN], gains, cosf, sinf)
    outs, lses = zip(*[_attn_group(qkv9, g) for g in range(N_GROUPS)])
    o_a = _attn_merge(outs, lses)

    qkv_d = _mm(h, w, mode="plain", out_dtype=F32, tn=512, col0=_C_GDN, n_out=3 * GDN_WIDTH)
    z = _mm(h, w, mode="silu", out_dtype=BF16, tn=512, col0=_C_Z, n_out=GDN_WIDTH)
    ba = _mm(h, w, mode="plain", out_dtype=F32, tn=512, col0=_C_BA, n_out=512)
    q_d, k_d, v_d = _gdn_prep(qkv_d, gdn_conv_w[l])
    ba_t = jnp.swapaxes(ba[:, :, :2 * GDN_HEADS], 1, 2)
    sc = _gdn_scalars(ba_t, gdn_a_log[l], gdn_dt_bias[l])
    sc = sc.reshape(bsz, 6, GDN_HEADS, s)
    cols = jnp.transpose(sc[:, :5], (0, 2, 3, 1))
    cols = jnp.concatenate([cols, jnp.zeros((bsz, GDN_HEADS, s, 3), F32)], axis=-1)
    grow = sc[:, 4].reshape(bsz, GDN_HEADS, 1, s)
    dcb = sc[:, 5, :, ::GDN_CHUNK].reshape(bsz, GDN_HEADS, s // GDN_TILE, GDN_TILE // GDN_CHUNK, 1)
    dcb = jnp.broadcast_to(dcb, dcb.shape[:-1] + (HEAD_DIM,))
    o_d = _gdn(q_d, k_d, v_d, z, cols, grow, dcb, gdn_norm_g[l].reshape(1, HEAD_DIM))

    u = _mm(h, w, mode="glu", out_dtype=F32, tn=512, col0=_C_U, col1=_C_U + CONV_CH, n_out=CONV_CH)
    dw = jnp.concatenate([conv_dw_w[l], jnp.zeros((1, CONV_CH), F32)], axis=0)
    o_c = _conformer(u, dw, conv_dw_b[l].reshape(1, CONV_CH), conv_ln_g[l].reshape(1, CONV_CH),
                     conv_ln_b[l].reshape(1, CONV_CH))

    return _merge(o_a, o_d, o_c, h, w_attn_o[l].astype(BF16), w_gdn_o[l].astype(BF16),
                  w_conv_o[l].astype(BF16), w, _C_GATE)


def kernel(x, c, positions, mix_mod_w, mix_mod_b, mix_norm_g, w_in, q_norm_g, k_norm_g, w_attn_o, gdn_conv_w, gdn_a_log, gdn_dt_bias, gdn_norm_g, w_gdn_o, conv_dw_w, conv_dw_b, conv_ln_g, conv_ln_b, w_conv_o, w_out, ffn_mod_w, ffn_mod_b, ffn_norm_g, w_gate_up, w_down):
    bsz, s, d = x.shape
    depth = w_in.shape[0]
    cosf, sinf = _rope_tables(positions)
    c_pad = jnp.concatenate([c, jnp.zeros((8 - bsz, d), c.dtype)], axis=0)
    mix_mods = _adaln_mods(c_pad, mix_mod_w, mix_mod_b)
    ffn_mods = _adaln_mods(c_pad, ffn_mod_w, ffn_mod_b)

    def mods(m, l):
        return tuple(m[l, :bsz, None, k * d:(k + 1) * d] for k in range(3))

    for l in range(depth):
        shift, scale, gate = mods(mix_mods, l)
        h = _mod_norm(x, mix_norm_g[l].reshape(1, d), scale, shift)
        m = _mixer(h, l, cosf, sinf, w_in, q_norm_g, k_norm_g, w_attn_o, gdn_conv_w, gdn_a_log,
                   gdn_dt_bias, gdn_norm_g, w_gdn_o, conv_dw_w, conv_dw_b, conv_ln_g, conv_ln_b, w_conv_o)
        x = _mm(m, w_out[l].astype(BF16), mode="residual", out_dtype=F32, tn=1024, x=x, gate=gate)
        shift, scale, gate = mods(ffn_mods, l)
        h = _mod_norm(x, ffn_norm_g[l].reshape(1, d), scale, shift)
        wgu = w_gate_up[l].astype(BF16)
        act = _mm(h, wgu, mode="swiglu", out_dtype=BF16, tn=512, col0=0, col1=D_FF, n_out=D_FF)
        x = _mm(act, w_down[l].astype(BF16), mode="residual", out_dtype=F32, tn=512, x=x, gate=gate)
    return x
```

```python
import functools

import jax
import jax.numpy as jnp
from jax import lax
from jax.experimental import pallas as pl
from jax.experimental.pallas import tpu as pltpu

F32 = jnp.float32
BF16 = jnp.bfloat16

D_MODEL = 2048
HEAD_DIM = 128
ROPE_DIM = HEAD_DIM // 4
ROPE_HALF = ROPE_DIM // 2
ROPE_THETA = 500000.0
DILATIONS = (1, 4, 16)
ATTN_BLOCK = 128
ATTN_NQ = (4, 2, 1)
N_GROUPS = 3
GROUP_HEADS = 4
GROUP_WIDTH = GROUP_HEADS * HEAD_DIM
ATTN_WIDTH = N_GROUPS * GROUP_WIDTH
GDN_HEADS = 8
GDN_WIDTH = GDN_HEADS * HEAD_DIM
GDN_CHUNK = 64
GDN_TILE = 256
SHORT_CONV = 4
CONV_CH = D_MODEL // 2
CONV_K = 31
D_FF = 5632
NORM_EPS = 1e-6
NEG_BIG = -1e30
SUBLANES = 8

VMEM_LIMIT_V7X = 56 * 1024 * 1024


def _params(sem, vmem=VMEM_LIMIT_V7X):
    return pltpu.CompilerParams(dimension_semantics=sem, vmem_limit_bytes=vmem)


def _silu(t):
    return t * jax.nn.sigmoid(t)


def _dot(a, b):
    return jnp.dot(a, b, preferred_element_type=F32)


def _dot_nt(a, b):
    return lax.dot_general(a, b, (((1,), (1,)), ((), ())), preferred_element_type=F32)


def _dot_tn(a, b):
    return lax.dot_general(a, b, (((0,), (0,)), ((), ())), preferred_element_type=F32)


def _mod_kernel(c_ref, w_ref, b_ref, o_ref):
    c = c_ref[...]
    o_ref[0] = _dot(_silu(c).astype(BF16), w_ref[0].astype(BF16)) + b_ref[0]


def _adaln_mods(c_pad, w, b):
    n_layers, d, n = w.shape
    tn = 1024
    return pl.pallas_call(
        _mod_kernel,
        grid=(n_layers, n // tn),
        in_specs=[
            pl.BlockSpec((SUBLANES, d), lambda l, j: (0, 0)),
            pl.BlockSpec((1, d, tn), lambda l, j: (l, 0, j)),
            pl.BlockSpec((1, 1, tn), lambda l, j: (l, 0, j)),
        ],
        out_specs=pl.BlockSpec((1, SUBLANES, tn), lambda l, j: (l, 0, j)),
        out_shape=jax.ShapeDtypeStruct((n_layers, SUBLANES, n), F32),
        compiler_params=_params(("parallel", "parallel")),
        name="adaln_mods",
    )(c_pad, w, b.reshape(n_layers, 1, n))


def _norm_kernel(x_ref, g_ref, sc_ref, sh_ref, o_ref):
    x = x_ref[0]
    ms = jnp.mean(x * x, axis=-1, keepdims=True)
    y = x * lax.rsqrt(ms + NORM_EPS) * g_ref[...]
    o_ref[0] = (y * (1.0 + sc_ref[0]) + sh_ref[0]).astype(BF16)


def _mod_norm(x, g, scale, shift):
    bsz, s, d = x.shape
    tm = 512
    vec = pl.BlockSpec((1, 1, d), lambda b, i: (b, 0, 0))
    return pl.pallas_call(
        _norm_kernel,
        grid=(bsz, s // tm),
        in_specs=[
            pl.BlockSpec((1, tm, d), lambda b, i: (b, i, 0)),
            pl.BlockSpec((1, d), lambda b, i: (0, 0)),
            vec, vec,
        ],
        out_specs=pl.BlockSpec((1, tm, d), lambda b, i: (b, i, 0)),
        out_shape=jax.ShapeDtypeStruct((bsz, s, d), BF16),
        compiler_params=_params(("parallel", "parallel")),
        name="mod_norm",
    )(x, g, scale, shift)


def _rope_kernel(pos_ref, inv_ref, sgn_ref, cos_ref, sin_ref):
    ang = pos_ref[0] * inv_ref[...]
    cos_ref[0] = jnp.cos(ang)
    sin_ref[0] = jnp.sin(ang) * sgn_ref[...]


def _rope_tables(positions):
    bsz, s = positions.shape
    tm = 1024
    inv = ROPE_THETA ** (-jnp.arange(0, ROPE_DIM, 2, dtype=F32) / ROPE_DIM)
    zeros = jnp.zeros((HEAD_DIM - ROPE_DIM,), F32)
    inv_full = jnp.concatenate([inv, inv, zeros]).reshape(1, HEAD_DIM)
    sgn = jnp.concatenate([-jnp.ones((ROPE_HALF,), F32), jnp.ones((ROPE_HALF,), F32), zeros])
    pos = positions.astype(F32).reshape(bsz, s, 1)
    out = jax.ShapeDtypeStruct((bsz, s, HEAD_DIM), F32)
    row = pl.BlockSpec((1, HEAD_DIM), lambda b, i: (0, 0))
    blk = pl.BlockSpec((1, tm, HEAD_DIM), lambda b, i: (b, i, 0))
    return pl.pallas_call(
        _rope_kernel,
        grid=(bsz, s // tm),
        in_specs=[pl.BlockSpec((1, tm, 1), lambda b, i: (b, i, 0)), row, row],
        out_specs=[blk, blk],
        out_shape=[out, out],
        compiler_params=_params(("parallel", "parallel")),
        name="rope_tables",
    )(pos, inv_full, sgn.reshape(1, HEAD_DIM))


def _qkv_kernel(a_ref, w_ref, g_ref, cos_ref, sin_ref, o0_ref, o1_ref, o2_ref, acc_scr, *, tm):
    j = pl.program_id(2)
    kind = j % 3
    group = j // 3
    acc = _dot(a_ref[0], w_ref[...])

    @pl.when(kind < 2)
    def _():
        cosf = cos_ref[0]
        sinf = sin_ref[0]
        lane = lax.broadcasted_iota(jnp.int32, (1, HEAD_DIM), 1)
        for hh in range(GROUP_HEADS):
            cols = slice(hh * HEAD_DIM, (hh + 1) * HEAD_DIM)
            t = acc[:, cols]
            ms = jnp.mean(t * t, axis=-1, keepdims=True)
            t = t * lax.rsqrt(ms + NORM_EPS) * g_ref[0]
            partner = jnp.where(lane < ROPE_HALF,
                                pltpu.roll(t, HEAD_DIM - ROPE_HALF, 1),
                                pltpu.roll(t, ROPE_HALF, 1))
            acc_scr[hh] = t * cosf + partner * sinf

    @pl.when(kind == 2)
    def _():
        for hh in range(GROUP_HEADS):
            acc_scr[hh] = acc[:, hh * HEAD_DIM:(hh + 1) * HEAD_DIM]

    for gi, (o_ref, r) in enumerate(zip((o0_ref, o1_ref, o2_ref), DILATIONS)):
        @pl.when(group == gi)
        def _(o_ref=o_ref, r=r):
            for rr in range(r):
                for hh in range(GROUP_HEADS):
                    c0 = rr * GROUP_WIDTH + hh * HEAD_DIM
                    rows = slice(None) if r == 1 else pl.ds(rr, tm // r, stride=r)
                    o_ref[0, 0, :, c0:c0 + HEAD_DIM] = acc_scr[hh, rows, :].astype(BF16)


def _proj_qkv(h, w_qkv, gains, cosf, sinf):
    bsz, s, k = h.shape
    tm, tn = 1024, GROUP_WIDTH

    def out_spec(g, r):
        return pl.BlockSpec((1, 1, tm // r, r * tn),
                            lambda b, i, j: (jnp.clip(j - 3 * g, 0, 2), b, i, 0))

    return pl.pallas_call(
        functools.partial(_qkv_kernel, tm=tm),
        grid=(bsz, s // tm, 3 * N_GROUPS),
        in_specs=[
            pl.BlockSpec((1, tm, k), lambda b, i, j: (b, i, 0)),
            pl.BlockSpec((k, tn), lambda b, i, j: (0, (j % 3) * N_GROUPS + j // 3)),
            pl.BlockSpec((1, 1, HEAD_DIM), lambda b, i, j: (j % 3, 0, 0)),
            pl.BlockSpec((1, tm, HEAD_DIM), lambda b, i, j: (b, i, 0)),
            pl.BlockSpec((1, tm, HEAD_DIM), lambda b, i, j: (b, i, 0)),
        ],
        out_specs=[out_spec(g, r) for g, r in enumerate(DILATIONS)],
        out_shape=[jax.ShapeDtypeStruct((3, bsz, s // r, r * tn), BF16) for r in DILATIONS],
        scratch_shapes=[pltpu.VMEM((GROUP_HEADS, tm, HEAD_DIM), F32)],
        compiler_params=_params(("parallel", "parallel", "arbitrary")),
        name="proj_qkv",
    )(h, w_qkv, gains, cosf, sinf)


def _mm_kernel(a_ref, *refs, mode):
    o_ref = refs[-1]
    a = a_ref[0]
    if mode == "plain":
        o_ref[0] = _dot(a, refs[0][...]).astype(o_ref.dtype)
    elif mode == "silu":
        o_ref[0] = _silu(_dot(a, refs[0][...])).astype(o_ref.dtype)
    elif mode == "glu":
        o_ref[0] = (_dot(a, refs[0][...]) * jax.nn.sigmoid(_dot(a, refs[1][...]))).astype(o_ref.dtype)
    elif mode == "swiglu":
        o_ref[0] = (_silu(_dot(a, refs[0][...])) * _dot(a, refs[1][...])).astype(o_ref.dtype)
    elif mode == "residual":
        x_ref, gate_ref = refs[1], refs[2]
        o_ref[0] = x_ref[0] + gate_ref[0] * _dot(a, refs[0][...])
    else:
        raise ValueError(mode)


def _mm(a, w, *, mode, out_dtype, tn, tm=1024, col1=0, n_out=None, x=None, gate=None):
    bsz, s, k = a.shape
    n_out = w.shape[1] if n_out is None else n_out
    assert col1 % tn == 0 and n_out % tn == 0 and s % tm == 0
    j1 = col1 // tn
    in_specs = [pl.BlockSpec((1, tm, k), lambda b, i, j: (b, i, 0)),
                pl.BlockSpec((k, tn), lambda b, i, j: (0, j))]
    args = [a, w]
    if mode in ("glu", "swiglu"):
        in_specs.append(pl.BlockSpec((k, tn), lambda b, i, j: (0, j1 + j)))
        args.append(w)
    if mode == "residual":
        in_specs += [pl.BlockSpec((1, tm, tn), lambda b, i, j: (b, i, j)),
                     pl.BlockSpec((1, 1, tn), lambda b, i, j: (b, 0, j))]
        args += [x, gate]
    return pl.pallas_call(
        functools.partial(_mm_kernel, mode=mode),
        grid=(bsz, s // tm, n_out // tn),
        in_specs=in_specs,
        out_specs=pl.BlockSpec((1, tm, tn), lambda b, i, j: (b, i, j)),
        out_shape=jax.ShapeDtypeStruct((bsz, s, n_out), out_dtype),
        compiler_params=_params(("parallel", "parallel", "arbitrary")),
        name="mm_" + mode,
    )(*args)


def _attn_kernel(q_ref, kc_ref, kp_ref, vc_ref, vp_ref, o_ref, l_ref, o_scr, *, r, nq):
    n = pl.program_id(1)
    qi = lax.broadcasted_iota(jnp.int32, (ATTN_BLOCK, ATTN_BLOCK), 0)
    kj = lax.broadcasted_iota(jnp.int32, (ATTN_BLOCK, ATTN_BLOCK), 1)
    cur_ok = kj <= qi
    prev_band = kj >= qi
    prev_first = jnp.logical_and(prev_band, n > 0)
    for rr in range(r):
        for t in range(nq):
            rows = slice(t * ATTN_BLOCK, (t + 1) * ATTN_BLOCK)
            prows = slice((t - 1) * ATTN_BLOCK, t * ATTN_BLOCK)
            prev_ok = prev_first if t == 0 else prev_band
            for hh in range(GROUP_HEADS):
                cols = slice(rr * GROUP_WIDTH + hh * HEAD_DIM, rr * GROUP_WIDTH + (hh + 1) * HEAD_DIM)
                q = q_ref[0, 0, rows, cols]
                kc = kc_ref[0, 0, rows, cols]
                vc = vc_ref[0, 0, rows, cols]
                if t == 0:
                    kp = kp_ref[0, 0, :, cols]
                    vp = vp_ref[0, 0, :, cols]
                else:
                    kp = kc_ref[0, 0, prows, cols]
                    vp = vc_ref[0, 0, prows, cols]
                s_c = jnp.where(cur_ok, _dot_nt(q, kc), NEG_BIG)
                s_p = jnp.where(prev_ok, _dot_nt(q, kp), NEG_BIG)
                m = jnp.maximum(jnp.max(s_c, axis=-1, keepdims=True),
                                jnp.max(s_p, axis=-1, keepdims=True))
                p_c = jnp.exp(s_c - m)
                p_p = jnp.exp(s_p - m)
                den = jnp.sum(p_c, axis=-1, keepdims=True) + jnp.sum(p_p, axis=-1, keepdims=True)
                o = (_dot(p_c.astype(BF16), vc) + _dot(p_p.astype(BF16), vp)) / den
                if r == 1:
                    o_scr[hh, rows, :] = o
                else:
                    o_scr[hh, pl.ds(t * ATTN_BLOCK * r + rr, ATTN_BLOCK, stride=r), :] = o
                lcol = rr * GROUP_HEADS + hh
                l_ref[0, rows, lcol:lcol + 1] = m + jnp.log(den)
    for hh in range(GROUP_HEADS):
        o_ref[0, :, hh * HEAD_DIM:(hh + 1) * HEAD_DIM] = o_scr[hh].astype(BF16)


def _attn_group(qkv, g):
    _, bsz, l, rw = qkv.shape
    r, nq = DILATIONS[g], ATTN_NQ[g]
    tq = ATTN_BLOCK * nq
    tp = tq * r
    s = l * r

    def cur(kind):
        return pl.BlockSpec((1, 1, tq, rw), lambda b, n: (kind, b, n, 0))

    def prev(kind):
        return pl.BlockSpec((1, 1, ATTN_BLOCK, rw), lambda b, n: (kind, b, jnp.maximum(n * nq - 1, 0), 0))

    o, lse = pl.pallas_call(
        functools.partial(_attn_kernel, r=r, nq=nq),
        grid=(bsz, l // tq),
        in_specs=[cur(0), cur(1), prev(1), cur(2), prev(2)],
        out_specs=[pl.BlockSpec((1, tp, GROUP_WIDTH), lambda b, n: (b, n, 0)),
                   pl.BlockSpec((1, tq, r * GROUP_HEADS), lambda b, n: (b, n, 0))],
        out_shape=[jax.ShapeDtypeStruct((bsz, s, GROUP_WIDTH), BF16),
                   jax.ShapeDtypeStruct((bsz, l, r * GROUP_HEADS), F32)],
        scratch_shapes=[pltpu.VMEM((GROUP_HEADS, tp, HEAD_DIM), F32)],
        compiler_params=_params(("parallel", "arbitrary")),
        name=f"attn_g{g}",
    )(qkv, qkv, qkv, qkv, qkv)
    return o, lse.reshape(bsz, s, GROUP_HEADS)


def _attn_merge_kernel(o0, o1, o2, l0, l1, l2, out_ref):
    a0, a1, a2 = l0[0], l1[0], l2[0]
    m = jnp.maximum(jnp.maximum(a0, a1), a2)
    e0, e1, e2 = jnp.exp(a0 - m), jnp.exp(a1 - m), jnp.exp(a2 - m)
    inv = 1.0 / (e0 + e1 + e2)
    w0, w1, w2 = e0 * inv, e1 * inv, e2 * inv
    for hh in range(GROUP_HEADS):
        cols = slice(hh * HEAD_DIM, (hh + 1) * HEAD_DIM)
        hc = slice(hh, hh + 1)
        acc = w0[:, hc] * o0[0, :, cols].astype(F32)
        acc = acc + w1[:, hc] * o1[0, :, cols].astype(F32)
        acc = acc + w2[:, hc] * o2[0, :, cols].astype(F32)
        out_ref[0, :, cols] = acc.astype(BF16)


def _attn_merge(outs, lses):
    bsz, s, w = outs[0].shape
    tm = 1024
    blk = pl.BlockSpec((1, tm, w), lambda b, i: (b, i, 0))
    lblk = pl.BlockSpec((1, tm, GROUP_HEADS), lambda b, i: (b, i, 0))
    return pl.pallas_call(
        _attn_merge_kernel,
        grid=(bsz, s // tm),
        in_specs=[blk] * 3 + [lblk] * 3,
        out_specs=blk,
        out_shape=jax.ShapeDtypeStruct((bsz, s, w), BF16),
        compiler_params=_params(("parallel", "parallel")),
        name="attn_merge",
    )(*outs, *lses)


def _conformer_kernel(u_ref, up_ref, w_ref, b_ref, g_ref, beta_ref, o_ref, scr, y_scr, *, tm):
    i = pl.program_id(1)
    pad = 32
    scr[0:pad, :] = jnp.where(i > 0, up_ref[0], 0.0)
    scr[pad:pad + tm, :] = u_ref[0]
    for c in range(CONV_CH // HEAD_DIM):
        cols = slice(c * HEAD_DIM, (c + 1) * HEAD_DIM)
        acc = None
        for kk in range(CONV_K):
            term = w_ref[kk:kk + 1, cols] * scr[pl.ds(pad - CONV_K + 1 + kk, tm), cols]
            acc = term if acc is None else acc + term
        y_scr[:, cols] = acc + b_ref[:, cols]
    y = y_scr[...]
    mu = jnp.mean(y, axis=-1, keepdims=True)
    yc = y - mu
    var = jnp.mean(yc * yc, axis=-1, keepdims=True)
    t = yc * lax.rsqrt(var + NORM_EPS) * g_ref[...] + beta_ref[...]
    o_ref[0] = _silu(t).astype(BF16)


def _conformer(u, w, b, g, beta):
    bsz, s, n = u.shape
    tm = 256
    row = pl.BlockSpec((1, n), lambda bb, i: (0, 0))
    return pl.pallas_call(
        functools.partial(_conformer_kernel, tm=tm),
        grid=(bsz, s // tm),
        in_specs=[
            pl.BlockSpec((1, tm, n), lambda bb, i: (bb, i, 0)),
            pl.BlockSpec((1, 32, n), lambda bb, i: (bb, jnp.maximum(i * (tm // 32) - 1, 0), 0)),
            pl.BlockSpec((32, n), lambda bb, i: (0, 0)),
            row, row, row,
        ],
        out_specs=pl.BlockSpec((1, tm, n), lambda bb, i: (bb, i, 0)),
        out_shape=jax.ShapeDtypeStruct((bsz, s, n), BF16),
        scratch_shapes=[pltpu.VMEM((tm + 32, n), F32), pltpu.VMEM((tm, n), F32)],
        compiler_params=_params(("parallel", "arbitrary")),
        name="conformer_conv",
    )(u, u, w, b, g, beta)


def _gdn_scalar_kernel(h_ref, w_ref, alog_ref, dtb_ref, o_ref, *, ts):
    x = _dot_nt(w_ref[...], h_ref[0])
    beta = jax.nn.sigmoid(x[0:GDN_HEADS])
    g = -jnp.exp(alog_ref[...]) * jax.nn.softplus(x[GDN_HEADS:2 * GDN_HEADS] + dtb_ref[...])
    lane = lax.broadcasted_iota(jnp.int32, (GDN_HEADS, ts), 1) % GDN_CHUNK
    fwd, bwd = g, g
    sh = 1
    while sh < GDN_CHUNK:
        fwd = fwd + jnp.where(lane >= sh, pltpu.roll(fwd, sh, 1), 0.0)
        bwd = bwd + jnp.where(lane < GDN_CHUNK - sh, pltpu.roll(bwd, ts - sh, 1), 0.0)
        sh *= 2
    rest = bwd - g
    eg = jnp.exp(fwd)
    o_ref[0, 0 * GDN_HEADS:1 * GDN_HEADS] = beta
    o_ref[0, 1 * GDN_HEADS:2 * GDN_HEADS] = eg
    o_ref[0, 2 * GDN_HEADS:3 * GDN_HEADS] = beta * eg
    o_ref[0, 3 * GDN_HEADS:4 * GDN_HEADS] = jnp.exp(rest)
    o_ref[0, 4 * GDN_HEADS:5 * GDN_HEADS] = fwd
    o_ref[0, 5 * GDN_HEADS:6 * GDN_HEADS] = jnp.exp(fwd + rest)


def _gdn_scalars(h, w_ba_t, a_log, dt_bias):
    bsz, s, d = h.shape
    ts = 2048
    col = pl.BlockSpec((GDN_HEADS, 1), lambda b, i: (0, 0))
    return pl.pallas_call(
        functools.partial(_gdn_scalar_kernel, ts=ts),
        grid=(bsz, s // ts),
        in_specs=[pl.BlockSpec((1, ts, d), lambda b, i: (b, i, 0)),
                  pl.BlockSpec((2 * GDN_HEADS, d), lambda b, i: (0, 0)), col, col],
        out_specs=pl.BlockSpec((1, 6 * GDN_HEADS, ts), lambda b, i: (b, 0, i)),
        out_shape=jax.ShapeDtypeStruct((bsz, 6 * GDN_HEADS, s), F32),
        compiler_params=_params(("parallel", "parallel")),
        name="gdn_scalars",
    )(h, w_ba_t, a_log.reshape(GDN_HEADS, 1), dt_bias.reshape(GDN_HEADS, 1))


def _gdn_kernel(x_ref, cw_ref, z_ref, col_ref, grow_ref, dc_ref, ng_ref, o_ref, xs, state):
    i = pl.program_id(1)
    ts = GDN_TILE
    pad = SUBLANES

    @pl.when(i == 0)
    def _():
        xs[0:pad, :] = jnp.zeros((pad, xs.shape[1]), F32)
        state[...] = jnp.zeros_like(state)

    @pl.when(i > 0)
    def _():
        xs[0:pad, :] = xs[ts:ts + pad, :]

    xs[pad:pad + ts, :] = x_ref[0]

    ri = lax.broadcasted_iota(jnp.int32, (ts, ts), 0)
    ci = lax.broadcasted_iota(jnp.int32, (ts, ts), 1)
    same = (ri // GDN_CHUNK) == (ci // GDN_CHUNK)
    causal = jnp.logical_and(same, ri >= ci)
    strict = jnp.logical_and(same, ri > ci)

    def conv(c0):
        cols = slice(c0, c0 + HEAD_DIM)
        acc = None
        for kk in range(SHORT_CONV):
            term = cw_ref[kk:kk + 1, cols] * xs[pl.ds(pad - SHORT_CONV + 1 + kk, ts), cols]
            acc = term if acc is None else acc + term
        return _silu(acc)

    def l2n(t):
        return t * lax.rsqrt(jnp.sum(t * t, axis=-1, keepdims=True) + NORM_EPS)

    for h in range(GDN_HEADS):
        hcols = slice(h * HEAD_DIM, (h + 1) * HEAD_DIM)
        q = l2n(conv(h * HEAD_DIM)) * (HEAD_DIM ** -0.5)
        k = l2n(conv(GDN_WIDTH + h * HEAD_DIM))
        v = conv(2 * GDN_WIDTH + h * HEAD_DIM)
        beta, gam, bgam, kdec, gcol = (col_ref[0, :, c * GDN_HEADS + h:c * GDN_HEADS + h + 1]
                                       for c in range(5))
        grow = grow_ref[0, h:h + 1, :]

        kb = k.astype(BF16)
        decay = jnp.exp(jnp.where(causal, gcol - grow, NEG_BIG))
        a = jnp.where(strict, beta * _dot_nt(kb, kb) * decay, 0.0)
        qkb = (_dot_nt(q.astype(BF16), kb) * decay).astype(BF16)

        p = a.astype(BF16)
        sol = jnp.concatenate([v * beta, k * bgam], axis=1)
        sol = sol - _dot(p, sol.astype(BF16))
        for _ in range(5):
            p = _dot(p, p).astype(BF16)
            sol = sol + _dot(p, sol.astype(BF16))
        u = sol[:, :HEAD_DIM]
        wb = sol[:, HEAD_DIM:].astype(BF16)

        qd = (q * gam).astype(BF16)
        kd = (k * kdec).astype(BF16)
        st = state[h]
        outs = []
        for c in range(ts // GDN_CHUNK):
            rows = slice(c * GDN_CHUNK, (c + 1) * GDN_CHUNK)
            sb = st.astype(BF16)
            v_new = u[rows] - _dot(wb[rows], sb)
            vb = v_new.astype(BF16)
            outs.append(_dot(qd[rows], sb) + _dot(qkb[rows, rows], vb))
            st = st * dc_ref[0, 0, c:c + 1, hcols] + _dot_tn(kd[rows], vb)
        state[h] = st
        o = jnp.concatenate(outs, axis=0)
        o = o * lax.rsqrt(jnp.mean(o * o, axis=-1, keepdims=True) + NORM_EPS) * ng_ref[...]
        o_ref[0, :, hcols] = (o * z_ref[0, :, hcols].astype(F32)).astype(BF16)


def _gdn(qkv_d, conv_w, z, cols, grow, dcb, norm_g):
    bsz, s, n = qkv_d.shape
    ts = GDN_TILE
    return pl.pallas_call(
        _gdn_kernel,
        grid=(bsz, s // ts),
        in_specs=[
            pl.BlockSpec((1, ts, n), lambda b, i: (b, i, 0)),
            pl.BlockSpec((SHORT_CONV, n), lambda b, i: (0, 0)),
            pl.BlockSpec((1, ts, GDN_WIDTH), lambda b, i: (b, i, 0)),
            pl.BlockSpec((1, ts, 5 * GDN_HEADS), lambda b, i: (b, i, 0)),
            pl.BlockSpec((1, GDN_HEADS, ts), lambda b, i: (b, 0, i)),
            pl.BlockSpec((1, 1, ts // GDN_CHUNK, GDN_WIDTH), lambda b, i: (b, i, 0, 0)),
            pl.BlockSpec((1, HEAD_DIM), lambda b, i: (0, 0)),
        ],
        out_specs=pl.BlockSpec((1, ts, GDN_WIDTH), lambda b, i: (b, i, 0)),
        out_shape=jax.ShapeDtypeStruct((bsz, s, GDN_WIDTH), BF16),
        scratch_shapes=[pltpu.VMEM((ts + SUBLANES, n), F32),
                        pltpu.VMEM((GDN_HEADS, HEAD_DIM, HEAD_DIM), F32)],
        compiler_params=_params(("parallel", "arbitrary")),
        name="gdn_delta",
    )(qkv_d, conv_w, z, cols, grow, dcb, norm_g)


def _merge_kernel(oa_ref, od_ref, oc_ref, h_ref, wa_ref, wd_ref, wc_ref, ga_ref, gb_ref, gc_ref, o_ref):
    h = h_ref[0]
    acc = jax.nn.sigmoid(_dot(h, ga_ref[...])) * _dot(oa_ref[0], wa_ref[...])
    acc = acc + jax.nn.sigmoid(_dot(h, gb_ref[...])) * _dot(od_ref[0], wd_ref[...])
    acc = acc + jax.nn.sigmoid(_dot(h, gc_ref[...])) * _dot(oc_ref[0], wc_ref[...])
    o_ref[0] = acc.astype(BF16)


def _merge(o_a, o_d, o_c, h, w_a, w_d, w_c, w_gate):
    bsz, s, d = h.shape
    tm, tn = 1024, 256
    nd = d // tn

    def rows(width):
        return pl.BlockSpec((1, tm, width), lambda b, i, j: (b, i, 0))

    def wcols(kdim, off=0):
        return pl.BlockSpec((kdim, tn), lambda b, i, j: (0, off + j))

    return pl.pallas_call(
        _merge_kernel,
        grid=(bsz, s // tm, nd),
        in_specs=[rows(o_a.shape[2]), rows(o_d.shape[2]), rows(o_c.shape[2]), rows(d),
                  wcols(w_a.shape[0]), wcols(w_d.shape[0]), wcols(w_c.shape[0]),
                  wcols(d), wcols(d, nd), wcols(d, 2 * nd)],
        out_specs=pl.BlockSpec((1, tm, tn), lambda b, i, j: (b, i, j)),
        out_shape=jax.ShapeDtypeStruct((bsz, s, d), BF16),
        compiler_params=_params(("parallel", "parallel", "arbitrary")),
        name="branch_merge",
    )(o_a, o_d, o_c, h, w_a, w_d, w_c, w_gate, w_gate, w_gate)


_C_GDN = 3 * ATTN_WIDTH
_C_BA = _C_GDN + 3 * GDN_WIDTH
_C_Z = _C_BA + 2 * GDN_HEADS
_C_U = _C_Z + GDN_WIDTH
_C_GATE = _C_U + 2 * CONV_CH
_C_END = _C_GATE + 3 * D_MODEL


def _mixer(h, l, cosf, sinf, w_in, q_norm_g, k_norm_g, w_attn_o, gdn_conv_w, gdn_a_log, gdn_dt_bias,
           gdn_norm_g, w_gdn_o, conv_dw_w, conv_dw_b, conv_ln_g, conv_ln_b, w_conv_o):
    bsz, s, _ = h.shape
    w = w_in[l]
    w_qkv = w[:, :_C_GDN].astype(BF16)
    w_d = w[:, _C_GDN:_C_BA].astype(BF16)
    w_ba_t = w[:, _C_BA:_C_Z].T.astype(BF16)
    w_z = w[:, _C_Z:_C_U].astype(BF16)
    w_u = w[:, _C_U:_C_GATE].astype(BF16)
    w_gate = w[:, _C_GATE:_C_END].astype(BF16)

    gains = jnp.stack([q_norm_g[l] * (HEAD_DIM ** -0.5), k_norm_g[l],
                       jnp.ones((HEAD_DIM,), F32)]).reshape(3, 1, HEAD_DIM)
    qkv_groups = _proj_qkv(h, w_qkv, gains, cosf, sinf)
    outs, lses = zip(*[_attn_group(qkv_groups[g], g) for g in range(N_GROUPS)])
    o_a = _attn_merge(outs, lses)

    qkv_d = _mm(h, w_d, mode="plain", out_dtype=F32, tn=512)
    z = _mm(h, w_z, mode="silu", out_dtype=BF16, tn=512)
    sc = _gdn_scalars(h, w_ba_t, gdn_a_log[l], gdn_dt_bias[l])
    cols = jnp.swapaxes(sc[:, :5 * GDN_HEADS], 1, 2)
    grow = sc[:, 4 * GDN_HEADS:5 * GDN_HEADS]
    dcb = jnp.swapaxes(sc[:, 5 * GDN_HEADS:, ::GDN_CHUNK], 1, 2)
    dcb = jnp.broadcast_to(dcb[..., None], dcb.shape + (HEAD_DIM,))
    dcb = dcb.reshape(bsz, s // GDN_TILE, GDN_TILE // GDN_CHUNK, GDN_WIDTH)
    o_d = _gdn(qkv_d, gdn_conv_w[l], z, cols, grow, dcb, gdn_norm_g[l].reshape(1, HEAD_DIM))

    u = _mm(h, w_u, mode="glu", out_dtype=F32, tn=512, col1=CONV_CH, n_out=CONV_CH)
    dw = jnp.concatenate([conv_dw_w[l], jnp.zeros((1, CONV_CH), F32)], axis=0)
    o_c = _conformer(u, dw, conv_dw_b[l].reshape(1, CONV_CH), conv_ln_g[l].reshape(1, CONV_CH),
                     conv_ln_b[l].reshape(1, CONV_CH))

    return _merge(o_a, o_d, o_c, h, w_attn_o[l].astype(BF16), w_gdn_o[l].astype(BF16),
                  w_conv_o[l].astype(BF16), w_gate)


def kernel(x, c, positions, mix_mod_w, mix_mod_b, mix_norm_g, w_in, q_norm_g, k_norm_g, w_attn_o, gdn_conv_w, gdn_a_log, gdn_dt_bias, gdn_norm_g, w_gdn_o, conv_dw_w, conv_dw_b, conv_ln_g, conv_ln_b, w_conv_o, w_out, ffn_mod_w, ffn_mod_b, ffn_norm_g, w_gate_up, w_down):
    bsz, s, d = x.shape
    depth = w_in.shape[0]
    cosf, sinf = _rope_tables(positions)
    c_pad = jnp.concatenate([c, jnp.zeros((SUBLANES - bsz, d), c.dtype)], axis=0)
    mix_mods = _adaln_mods(c_pad, mix_mod_w, mix_mod_b)
    ffn_mods = _adaln_mods(c_pad, ffn_mod_w, ffn_mod_b)

    def mods(m, l):
        return tuple(m[l, :bsz, None, k * d:(k + 1) * d] for k in range(3))

    for l in range(depth):
        shift, scale, gate = mods(mix_mods, l)
        h = _mod_norm(x, mix_norm_g[l].reshape(1, d), scale, shift)
        m = _mixer(h, l, cosf, sinf, w_in, q_norm_g, k_norm_g, w_attn_o, gdn_conv_w, gdn_a_log,
                   gdn_dt_bias, gdn_norm_g, w_gdn_o, conv_dw_w, conv_dw_b, conv_ln_g, conv_ln_b, w_conv_o)
        x = _mm(m, w_out[l].astype(BF16), mode="residual", out_dtype=F32, tn=1024, x=x, gate=gate)
        shift, scale, gate = mods(ffn_mods, l)
        h = _mod_norm(x, ffn_norm_g[l].reshape(1, d), scale, shift)
        wgu = w_gate_up[l].astype(BF16)
        act = _mm(h, wgu, mode="swiglu", out_dtype=BF16, tn=512, col1=D_FF, n_out=D_FF)
        x = _mm(act, w_down[l].astype(BF16), mode="residual", out_dtype=F32, tn=512, x=x, gate=gate)
    return x
```

```python
import functools

import jax
import jax.numpy as jnp
from jax import lax
from jax.experimental import pallas as pl
from jax.experimental.pallas import tpu as pltpu

F32 = jnp.float32
BF16 = jnp.bfloat16

D_MODEL = 2048
HEAD_DIM = 128
ROPE_DIM = HEAD_DIM // 4
ROPE_HALF = ROPE_DIM // 2
ROPE_THETA = 500000.0
DILATIONS = (1, 4, 16)
ATTN_BLOCK = 128
ATTN_NQ = (4, 2, 1)
QKV_SUBTILES = 4
N_GROUPS = 3
GROUP_HEADS = 4
GROUP_WIDTH = GROUP_HEADS * HEAD_DIM
ATTN_WIDTH = N_GROUPS * GROUP_WIDTH
GDN_HEADS = 8
GDN_WIDTH = GDN_HEADS * HEAD_DIM
GDN_CHUNK = 256
GDN_TILE = GDN_CHUNK
GDN_NEUMANN_SQUARINGS = 7
GDN_LOCKSTEP = 4
SHORT_CONV = 4
CONV_CH = D_MODEL // 2
CONV_K = 31
D_FF = 5632
NORM_EPS = 1e-6
NEG_BIG = -1e30
SUBLANES = 8

VMEM_LIMIT_V7X = 56 * 1024 * 1024


def _params(sem, vmem=VMEM_LIMIT_V7X):
    return pltpu.CompilerParams(dimension_semantics=sem, vmem_limit_bytes=vmem)


def _silu(t):
    return t * jax.nn.sigmoid(t)


def _dot(a, b):
    return jnp.dot(a, b, preferred_element_type=F32)


def _dot_nt(a, b):
    return lax.dot_general(a, b, (((1,), (1,)), ((), ())), preferred_element_type=F32)


def _dot_tn(a, b):
    return lax.dot_general(a, b, (((0,), (0,)), ((), ())), preferred_element_type=F32)


def _mod_kernel(c_ref, w_ref, b_ref, o_ref):
    c = c_ref[...]
    o_ref[0] = _dot(_silu(c).astype(BF16), w_ref[0].astype(BF16)) + b_ref[0]


def _adaln_mods(c_pad, w, b):
    n_layers, d, n = w.shape
    tn = 1024
    return pl.pallas_call(
        _mod_kernel,
        grid=(n_layers, n // tn),
        in_specs=[
            pl.BlockSpec((SUBLANES, d), lambda l, j: (0, 0)),
            pl.BlockSpec((1, d, tn), lambda l, j: (l, 0, j)),
            pl.BlockSpec((1, 1, tn), lambda l, j: (l, 0, j)),
        ],
        out_specs=pl.BlockSpec((1, SUBLANES, tn), lambda l, j: (l, 0, j)),
        out_shape=jax.ShapeDtypeStruct((n_layers, SUBLANES, n), F32),
        compiler_params=_params(("parallel", "parallel")),
        name="adaln_mods",
    )(c_pad, w, b.reshape(n_layers, 1, n))


def _norm_kernel(x_ref, g_ref, sc_ref, sh_ref, o_ref):
    x = x_ref[0]
    ms = jnp.mean(x * x, axis=-1, keepdims=True)
    y = x * lax.rsqrt(ms + NORM_EPS) * g_ref[...]
    o_ref[0] = (y * (1.0 + sc_ref[0]) + sh_ref[0]).astype(BF16)


def _mod_norm(x, g, scale, shift):
    bsz, s, d = x.shape
    tm = 512
    vec = pl.BlockSpec((1, 1, d), lambda b, i: (b, 0, 0))
    return pl.pallas_call(
        _norm_kernel,
        grid=(bsz, s // tm),
        in_specs=[
            pl.BlockSpec((1, tm, d), lambda b, i: (b, i, 0)),
            pl.BlockSpec((1, d), lambda b, i: (0, 0)),
            vec, vec,
        ],
        out_specs=pl.BlockSpec((1, tm, d), lambda b, i: (b, i, 0)),
        out_shape=jax.ShapeDtypeStruct((bsz, s, d), BF16),
        compiler_params=_params(("parallel", "parallel")),
        name="mod_norm",
    )(x, g, scale, shift)


def _rope_kernel(pos_ref, inv_ref, sgn_ref, cos_ref, sin_ref):
    ang = pos_ref[0] * inv_ref[...]
    cos_ref[0] = jnp.cos(ang)
    sin_ref[0] = jnp.sin(ang) * sgn_ref[...]


def _rope_tables(positions):
    bsz, s = positions.shape
    tm = 1024
    inv = ROPE_THETA ** (-jnp.arange(0, ROPE_DIM, 2, dtype=F32) / ROPE_DIM)
    zeros = jnp.zeros((HEAD_DIM - ROPE_DIM,), F32)
    inv_full = jnp.concatenate([inv, inv, zeros]).reshape(1, HEAD_DIM)
    sgn = jnp.concatenate([-jnp.ones((ROPE_HALF,), F32), jnp.ones((ROPE_HALF,), F32), zeros])
    pos = positions.astype(F32).reshape(bsz, s, 1)
    out = jax.ShapeDtypeStruct((bsz, s, HEAD_DIM), F32)
    row = pl.BlockSpec((1, HEAD_DIM), lambda b, i: (0, 0))
    blk = pl.BlockSpec((1, tm, HEAD_DIM), lambda b, i: (b, i, 0))
    return pl.pallas_call(
        _rope_kernel,
        grid=(bsz, s // tm),
        in_specs=[pl.BlockSpec((1, tm, 1), lambda b, i: (b, i, 0)), row, row],
        out_specs=[blk, blk],
        out_shape=[out, out],
        compiler_params=_params(("parallel", "parallel")),
        name="rope_tables",
    )(pos, inv_full, sgn.reshape(1, HEAD_DIM))


def _qkv_kernel(a_ref, w_ref, g_ref, cos_ref, sin_ref, o0_ref, o1_ref, o2_ref, acc_scr, *, tm):
    j = pl.program_id(2)
    kind = j % 3
    group = j // 3
    lane = lax.broadcasted_iota(jnp.int32, (1, HEAD_DIM), 1)
    sub = tm // QKV_SUBTILES

    def tile(o_ref, r, normed):
        for si in range(QKV_SUBTILES):
            rows = slice(si * sub, (si + 1) * sub)
            acc = _dot(a_ref[0, rows, :], w_ref[...])
            for hh in range(GROUP_HEADS):
                t = acc[:, hh * HEAD_DIM:(hh + 1) * HEAD_DIM]
                if normed:
                    ms = jnp.mean(t * t, axis=-1, keepdims=True)
                    t = t * lax.rsqrt(ms + NORM_EPS) * g_ref[0]
                    partner = jnp.where(lane < ROPE_HALF,
                                        pltpu.roll(t, HEAD_DIM - ROPE_HALF, 1),
                                        pltpu.roll(t, ROPE_HALF, 1))
                    t = t * cos_ref[0, rows, :] + partner * sin_ref[0, rows, :]
                if r == 1:
                    o_ref[0, 0, rows, hh * HEAD_DIM:(hh + 1) * HEAD_DIM] = t.astype(BF16)
                else:
                    acc_scr[hh, rows, :] = t
            if r > 1:
                orows = slice(si * sub // r, (si + 1) * sub // r)
                for rr in range(r):
                    for hh in range(GROUP_HEADS):
                        c0 = rr * GROUP_WIDTH + hh * HEAD_DIM
                        o_ref[0, 0, orows, c0:c0 + HEAD_DIM] = (
                            acc_scr[hh, pl.ds(si * sub + rr, sub // r, stride=r), :].astype(BF16))

    for gi, (o_ref, r) in enumerate(zip((o0_ref, o1_ref, o2_ref), DILATIONS)):
        for normed in (True, False):
            is_kind = (kind < 2) if normed else (kind == 2)
            pl.when(jnp.logical_and(group == gi, is_kind))(
                functools.partial(tile, o_ref, r, normed))


def _proj_qkv(h, w_qkv, gains, cosf, sinf):
    bsz, s, k = h.shape
    tm, tn = 1024, GROUP_WIDTH

    def out_spec(g, r):
        return pl.BlockSpec((1, 1, tm // r, r * tn),
                            lambda b, i, j: (jnp.clip(j - 3 * g, 0, 2), b, i, 0))

    return pl.pallas_call(
        functools.partial(_qkv_kernel, tm=tm),
        grid=(bsz, s // tm, 3 * N_GROUPS),
        in_specs=[
            pl.BlockSpec((1, tm, k), lambda b, i, j: (b, i, 0)),
            pl.BlockSpec((k, tn), lambda b, i, j: (0, (j % 3) * N_GROUPS + j // 3)),
            pl.BlockSpec((1, 1, HEAD_DIM), lambda b, i, j: (j % 3, 0, 0)),
            pl.BlockSpec((1, tm, HEAD_DIM), lambda b, i, j: (b, i, 0)),
            pl.BlockSpec((1, tm, HEAD_DIM), lambda b, i, j: (b, i, 0)),
        ],
        out_specs=[out_spec(g, r) for g, r in enumerate(DILATIONS)],
        out_shape=[jax.ShapeDtypeStruct((3, bsz, s // r, r * tn), BF16) for r in DILATIONS],
        scratch_shapes=[pltpu.VMEM((GROUP_HEADS, tm, HEAD_DIM), F32)],
        compiler_params=_params(("parallel", "parallel", "arbitrary")),
        name="proj_qkv",
    )(h, w_qkv, gains, cosf, sinf)


def _mm_kernel(a_ref, *refs, mode):
    o_ref = refs[-1]
    a = a_ref[0]
    if mode == "plain":
        o_ref[0] = _dot(a, refs[0][...]).astype(o_ref.dtype)
    elif mode == "silu":
        o_ref[0] = _silu(_dot(a, refs[0][...])).astype(o_ref.dtype)
    elif mode == "glu":
        o_ref[0] = (_dot(a, refs[0][...]) * jax.nn.sigmoid(_dot(a, refs[1][...]))).astype(o_ref.dtype)
    elif mode == "swiglu":
        o_ref[0] = (_silu(_dot(a, refs[0][...])) * _dot(a, refs[1][...])).astype(o_ref.dtype)
    elif mode == "residual":
        x_ref, gate_ref = refs[1], refs[2]
        o_ref[0] = x_ref[0] + gate_ref[0] * _dot(a, refs[0][...])
    else:
        raise ValueError(mode)


def _mm(a, w, *, mode, out_dtype, tn, tm=1024, col1=0, n_out=None, x=None, gate=None):
    bsz, s, k = a.shape
    n_out = w.shape[1] if n_out is None else n_out
    assert col1 % tn == 0 and n_out % tn == 0 and s % tm == 0
    j1 = col1 // tn
    in_specs = [pl.BlockSpec((1, tm, k), lambda b, i, j: (b, i, 0)),
                pl.BlockSpec((k, tn), lambda b, i, j: (0, j))]
    args = [a, w]
    if mode in ("glu", "swiglu"):
        in_specs.append(pl.BlockSpec((k, tn), lambda b, i, j: (0, j1 + j)))
        args.append(w)
    if mode == "residual":
        in_specs += [pl.BlockSpec((1, tm, tn), lambda b, i, j: (b, i, j)),
                     pl.BlockSpec((1, 1, tn), lambda b, i, j: (b, 0, j))]
        args += [x, gate]
    return pl.pallas_call(
        functools.partial(_mm_kernel, mode=mode),
        grid=(bsz, s // tm, n_out // tn),
        in_specs=in_specs,
        out_specs=pl.BlockSpec((1, tm, tn), lambda b, i, j: (b, i, j)),
        out_shape=jax.ShapeDtypeStruct((bsz, s, n_out), out_dtype),
        compiler_params=_params(("parallel", "parallel", "arbitrary")),
        name="mm_" + mode,
    )(*args)


def _attn_kernel(q_ref, kc_ref, kp_ref, vc_ref, vp_ref, o_ref, l_ref, o_scr, *, r, nq):
    n = pl.program_id(1)
    qi = lax.broadcasted_iota(jnp.int32, (ATTN_BLOCK, ATTN_BLOCK), 0)
    kj = lax.broadcasted_iota(jnp.int32, (ATTN_BLOCK, ATTN_BLOCK), 1)
    cur_ok = kj <= qi
    prev_band = kj >= qi
    prev_first = jnp.logical_and(prev_band, n > 0)
    for rr in range(r):
        for t in range(nq):
            rows = slice(t * ATTN_BLOCK, (t + 1) * ATTN_BLOCK)
            prows = slice((t - 1) * ATTN_BLOCK, t * ATTN_BLOCK)
            prev_ok = prev_first if t == 0 else prev_band
            heads = range(GROUP_HEADS)
            cols = [slice(rr * GROUP_WIDTH + hh * HEAD_DIM, rr * GROUP_WIDTH + (hh + 1) * HEAD_DIM)
                    for hh in heads]
            if t == 0:
                kps = [kp_ref[0, 0, :, cols[hh]] for hh in heads]
                vps = [vp_ref[0, 0, :, cols[hh]] for hh in heads]
            else:
                kps = [kc_ref[0, 0, prows, cols[hh]] for hh in heads]
                vps = [vc_ref[0, 0, prows, cols[hh]] for hh in heads]
            qs = [q_ref[0, 0, rows, cols[hh]] for hh in heads]
            s_c = [jnp.where(cur_ok, _dot_nt(qs[hh], kc_ref[0, 0, rows, cols[hh]]), NEG_BIG) for hh in heads]
            s_p = [jnp.where(prev_ok, _dot_nt(qs[hh], kps[hh]), NEG_BIG) for hh in heads]
            m = [jnp.maximum(jnp.max(s_c[hh], axis=-1, keepdims=True),
                             jnp.max(s_p[hh], axis=-1, keepdims=True)) for hh in heads]
            p_c = [jnp.exp(s_c[hh] - m[hh]) for hh in heads]
            p_p = [jnp.exp(s_p[hh] - m[hh]) for hh in heads]
            den = [jnp.sum(p_c[hh], axis=-1, keepdims=True) + jnp.sum(p_p[hh], axis=-1, keepdims=True)
                   for hh in heads]
            acc = [_dot(p_c[hh].astype(BF16), vc_ref[0, 0, rows, cols[hh]])
                   + _dot(p_p[hh].astype(BF16), vps[hh]) for hh in heads]
            for hh in heads:
                o = acc[hh] / den[hh]
                if r == 1:
                    o_scr[hh, rows, :] = o
                else:
                    o_scr[hh, pl.ds(t * ATTN_BLOCK * r + rr, ATTN_BLOCK, stride=r), :] = o
                lcol = rr * GROUP_HEADS + hh
                l_ref[0, rows, lcol:lcol + 1] = m[hh] + jnp.log(den[hh])
    for hh in range(GROUP_HEADS):
        o_ref[0, :, hh * HEAD_DIM:(hh + 1) * HEAD_DIM] = o_scr[hh].astype(BF16)


def _attn_group(qkv, g):
    _, bsz, l, rw = qkv.shape
    r, nq = DILATIONS[g], ATTN_NQ[g]
    tq = ATTN_BLOCK * nq
    tp = tq * r
    s = l * r

    def cur(kind):
        return pl.BlockSpec((1, 1, tq, rw), lambda b, n: (kind, b, n, 0))

    def prev(kind):
        return pl.BlockSpec((1, 1, ATTN_BLOCK, rw), lambda b, n: (kind, b, jnp.maximum(n * nq - 1, 0), 0))

    o, lse = pl.pallas_call(
        functools.partial(_attn_kernel, r=r, nq=nq),
        grid=(bsz, l // tq),
        in_specs=[cur(0), cur(1), prev(1), cur(2), prev(2)],
        out_specs=[pl.BlockSpec((1, tp, GROUP_WIDTH), lambda b, n: (b, n, 0)),
                   pl.BlockSpec((1, tq, r * GROUP_HEADS), lambda b, n: (b, n, 0))],
        out_shape=[jax.ShapeDtypeStruct((bsz, s, GROUP_WIDTH), BF16),
                   jax.ShapeDtypeStruct((bsz, l, r * GROUP_HEADS), F32)],
        scratch_shapes=[pltpu.VMEM((GROUP_HEADS, tp, HEAD_DIM), F32)],
        compiler_params=_params(("parallel", "arbitrary")),
        name=f"attn_g{g}",
    )(qkv, qkv, qkv, qkv, qkv)
    return o, lse.reshape(bsz, s, GROUP_HEADS)


def _attn_merge_kernel(o0, o1, o2, l0, l1, l2, out_ref):
    a0, a1, a2 = l0[0], l1[0], l2[0]
    m = jnp.maximum(jnp.maximum(a0, a1), a2)
    e0, e1, e2 = jnp.exp(a0 - m), jnp.exp(a1 - m), jnp.exp(a2 - m)
    inv = 1.0 / (e0 + e1 + e2)
    w0, w1, w2 = e0 * inv, e1 * inv, e2 * inv
    for hh in range(GROUP_HEADS):
        cols = slice(hh * HEAD_DIM, (hh + 1) * HEAD_DIM)
        hc = slice(hh, hh + 1)
        acc = w0[:, hc] * o0[0, :, cols].astype(F32)
        acc = acc + w1[:, hc] * o1[0, :, cols].astype(F32)
        acc = acc + w2[:, hc] * o2[0, :, cols].astype(F32)
        out_ref[0, :, cols] = acc.astype(BF16)


def _attn_merge(outs, lses):
    bsz, s, w = outs[0].shape
    tm = 1024
    blk = pl.BlockSpec((1, tm, w), lambda b, i: (b, i, 0))
    lblk = pl.BlockSpec((1, tm, GROUP_HEADS), lambda b, i: (b, i, 0))
    return pl.pallas_call(
        _attn_merge_kernel,
        grid=(bsz, s // tm),
        in_specs=[blk] * 3 + [lblk] * 3,
        out_specs=blk,
        out_shape=jax.ShapeDtypeStruct((bsz, s, w), BF16),
        compiler_params=_params(("parallel", "parallel")),
        name="attn_merge",
    )(*outs, *lses)


def _conformer_kernel(u_ref, up_ref, w_ref, b_ref, g_ref, beta_ref, o_ref, scr, sh_scr, y_scr, *, tm):
    i = pl.program_id(1)
    pad = 32
    scr[0:pad, :] = jnp.where(i > 0, up_ref[0], 0.0)
    scr[pad:pad + tm, :] = u_ref[0]
    for b in range(1, SUBLANES):
        sh_scr[b - 1] = scr[pl.ds(b, tm + pad - SUBLANES), :]
    for c in range(CONV_CH // HEAD_DIM):
        cols = slice(c * HEAD_DIM, (c + 1) * HEAD_DIM)
        acc = None
        for kk in range(CONV_K):
            a8, b = divmod(pad - CONV_K + 1 + kk, SUBLANES)
            src = scr if b == 0 else sh_scr.at[b - 1]
            term = w_ref[kk:kk + 1, cols] * src[pl.ds(a8 * SUBLANES, tm), cols]
            acc = term if acc is None else acc + term
        y_scr[:, cols] = acc + b_ref[:, cols]
    y = y_scr[...]
    mu = jnp.mean(y, axis=-1, keepdims=True)
    yc = y - mu
    var = jnp.mean(yc * yc, axis=-1, keepdims=True)
    t = yc * lax.rsqrt(var + NORM_EPS) * g_ref[...] + beta_ref[...]
    o_ref[0] = _silu(t).astype(BF16)


def _conformer(u, w, b, g, beta):
    bsz, s, n = u.shape
    tm = 256
    row = pl.BlockSpec((1, n), lambda bb, i: (0, 0))
    return pl.pallas_call(
        functools.partial(_conformer_kernel, tm=tm),
        grid=(bsz, s // tm),
        in_specs=[
            pl.BlockSpec((1, tm, n), lambda bb, i: (bb, i, 0)),
            pl.BlockSpec((1, 32, n), lambda bb, i: (bb, jnp.maximum(i * (tm // 32) - 1, 0), 0)),
            pl.BlockSpec((32, n), lambda bb, i: (0, 0)),
            row, row, row,
        ],
        out_specs=pl.BlockSpec((1, tm, n), lambda bb, i: (bb, i, 0)),
        out_shape=jax.ShapeDtypeStruct((bsz, s, n), BF16),
        scratch_shapes=[pltpu.VMEM((tm + 32, n), F32),
                        pltpu.VMEM((SUBLANES - 1, tm + 32 - SUBLANES, n), F32),
                        pltpu.VMEM((tm, n), F32)],
        compiler_params=_params(("parallel", "arbitrary")),
        name="conformer_conv",
    )(u, u, w, b, g, beta)


def _gdn_scalar_kernel(h_ref, w_ref, alog_ref, dtb_ref, o_ref, *, ts):
    x = _dot_nt(w_ref[...], h_ref[0])
    beta = jax.nn.sigmoid(x[0:GDN_HEADS])
    g = -jnp.exp(alog_ref[...]) * jax.nn.softplus(x[GDN_HEADS:2 * GDN_HEADS] + dtb_ref[...])
    lane = lax.broadcasted_iota(jnp.int32, (GDN_HEADS, ts), 1) % GDN_CHUNK
    fwd, bwd = g, g
    sh = 1
    while sh < GDN_CHUNK:
        fwd = fwd + jnp.where(lane >= sh, pltpu.roll(fwd, sh, 1), 0.0)
        bwd = bwd + jnp.where(lane < GDN_CHUNK - sh, pltpu.roll(bwd, ts - sh, 1), 0.0)
        sh *= 2
    rest = bwd - g
    eg = jnp.exp(fwd)
    o_ref[0, 0 * GDN_HEADS:1 * GDN_HEADS] = beta
    o_ref[0, 1 * GDN_HEADS:2 * GDN_HEADS] = eg
    o_ref[0, 2 * GDN_HEADS:3 * GDN_HEADS] = beta * eg
    o_ref[0, 3 * GDN_HEADS:4 * GDN_HEADS] = jnp.exp(rest)
    o_ref[0, 4 * GDN_HEADS:5 * GDN_HEADS] = fwd
    o_ref[0, 5 * GDN_HEADS:6 * GDN_HEADS] = jnp.exp(fwd + rest)


def _gdn_scalars(h, w_ba_t, a_log, dt_bias):
    bsz, s, d = h.shape
    ts = 2048
    col = pl.BlockSpec((GDN_HEADS, 1), lambda b, i: (0, 0))
    return pl.pallas_call(
        functools.partial(_gdn_scalar_kernel, ts=ts),
        grid=(bsz, s // ts),
        in_specs=[pl.BlockSpec((1, ts, d), lambda b, i: (b, i, 0)),
                  pl.BlockSpec((2 * GDN_HEADS, d), lambda b, i: (0, 0)), col, col],
        out_specs=pl.BlockSpec((1, 6 * GDN_HEADS, ts), lambda b, i: (b, 0, i)),
        out_shape=jax.ShapeDtypeStruct((bsz, 6 * GDN_HEADS, s), F32),
        compiler_params=_params(("parallel", "parallel")),
        name="gdn_scalars",
    )(h, w_ba_t, a_log.reshape(GDN_HEADS, 1), dt_bias.reshape(GDN_HEADS, 1))


def _gdn_kernel(x_ref, cw_ref, z_ref, col_ref, grow_ref, dc_ref, ng_ref, o_ref, xs, state):
    i = pl.program_id(1)
    ts = GDN_TILE
    pad = SUBLANES

    @pl.when(i == 0)
    def _():
        xs[0:pad, :] = jnp.zeros((pad, xs.shape[1]), F32)
        state[...] = jnp.zeros_like(state)

    @pl.when(i > 0)
    def _():
        xs[0:pad, :] = xs[ts:ts + pad, :]

    xs[pad:pad + ts, :] = x_ref[0]

    ri = lax.broadcasted_iota(jnp.int32, (ts, ts), 0)
    ci = lax.broadcasted_iota(jnp.int32, (ts, ts), 1)
    causal = ri >= ci
    strict = ri > ci

    def conv(c0):
        cols = slice(c0, c0 + HEAD_DIM)
        acc = None
        for kk in reversed(range(SHORT_CONV)):
            term = cw_ref[kk:kk + 1, cols] * xs[pl.ds(pad - SHORT_CONV + 1 + kk, ts), cols]
            acc = term if acc is None else acc + term
        return _silu(acc)

    def l2n(t):
        return t * lax.rsqrt(jnp.sum(t * t, axis=-1, keepdims=True) + NORM_EPS)

    for h0 in range(0, GDN_HEADS, GDN_LOCKSTEP):
        heads = range(h0, h0 + GDN_LOCKSTEP)
        pre = {}
        for h in heads:
            q = l2n(conv(h * HEAD_DIM)) * (HEAD_DIM ** -0.5)
            k = l2n(conv(GDN_WIDTH + h * HEAD_DIM))
            v = conv(2 * GDN_WIDTH + h * HEAD_DIM)
            beta, gam, bgam, kdec, gcol = (col_ref[0, :, c * GDN_HEADS + h:c * GDN_HEADS + h + 1]
                                           for c in range(5))
            grow = grow_ref[0, h:h + 1, :]
            ktb = k.T.astype(BF16)
            kb = k.astype(BF16)
            decay = jnp.exp(jnp.where(causal, gcol - grow, NEG_BIG))
            a = jnp.where(strict, beta * _dot(kb, ktb) * decay, 0.0)
            qkb = (_dot(q.astype(BF16), ktb) * decay).astype(BF16)
            sol = jnp.concatenate([v * beta, k * bgam], axis=1)
            pre[h] = dict(p=a.astype(BF16), sol=sol, qkb=qkb,
                          qd=(q * gam).astype(BF16), kdt=(k * kdec).T.astype(BF16))

        for h in heads:
            d = pre[h]
            d["sol"] = d["sol"] - _dot(d["p"], d["sol"].astype(BF16))
        for _ in range(GDN_NEUMANN_SQUARINGS):
            for h in heads:
                d = pre[h]
                d["p"] = _dot(d["p"], d["p"]).astype(BF16)
            for h in heads:
                d = pre[h]
                d["sol"] = d["sol"] + _dot(d["p"], d["sol"].astype(BF16))

        for h in heads:
            d = pre[h]
            hcols = slice(h * HEAD_DIM, (h + 1) * HEAD_DIM)
            st = state[h]
            sb = st.astype(BF16)
            v_new = d["sol"][:, :HEAD_DIM] - _dot(d["sol"][:, HEAD_DIM:].astype(BF16), sb)
            vb = v_new.astype(BF16)
            o = _dot(d["qd"], sb) + _dot(d["qkb"], vb)
            state[h] = st * dc_ref[0, 0, :, hcols] + _dot(d["kdt"], vb)
            o = o * lax.rsqrt(jnp.mean(o * o, axis=-1, keepdims=True) + NORM_EPS) * ng_ref[...]
            o_ref[0, :, hcols] = (o * z_ref[0, :, hcols].astype(F32)).astype(BF16)


def _gdn(qkv_d, conv_w, z, cols, grow, dcb, norm_g):
    bsz, s, n = qkv_d.shape
    ts = GDN_TILE
    return pl.pallas_call(
        _gdn_kernel,
        grid=(bsz, s // ts),
        in_specs=[
            pl.BlockSpec((1, ts, n), lambda b, i: (b, i, 0)),
            pl.BlockSpec((SHORT_CONV, n), lambda b, i: (0, 0)),
            pl.BlockSpec((1, ts, GDN_WIDTH), lambda b, i: (b, i, 0)),
            pl.BlockSpec((1, ts, 5 * GDN_HEADS), lambda b, i: (b, i, 0)),
            pl.BlockSpec((1, GDN_HEADS, ts), lambda b, i: (b, 0, i)),
            pl.BlockSpec((1, 1, ts // GDN_CHUNK, GDN_WIDTH), lambda b, i: (b, i, 0, 0)),
            pl.BlockSpec((1, HEAD_DIM), lambda b, i: (0, 0)),
        ],
        out_specs=pl.BlockSpec((1, ts, GDN_WIDTH), lambda b, i: (b, i, 0)),
        out_shape=jax.ShapeDtypeStruct((bsz, s, GDN_WIDTH), BF16),
        scratch_shapes=[pltpu.VMEM((ts + SUBLANES, n), F32),
                        pltpu.VMEM((GDN_HEADS, HEAD_DIM, HEAD_DIM), F32)],
        compiler_params=_params(("parallel", "arbitrary")),
        name="gdn_delta",
    )(qkv_d, conv_w, z, cols, grow, dcb, norm_g)


def _merge_kernel(oa_ref, od_ref, oc_ref, h_ref, wa_ref, wd_ref, wc_ref, ga_ref, gb_ref, gc_ref, o_ref):
    h = h_ref[0]
    acc = jax.nn.sigmoid(_dot(h, ga_ref[...])) * _dot(oa_ref[0], wa_ref[...])
    acc = acc + jax.nn.sigmoid(_dot(h, gb_ref[...])) * _dot(od_ref[0], wd_ref[...])
    acc = acc + jax.nn.sigmoid(_dot(h, gc_ref[...])) * _dot(oc_ref[0], wc_ref[...])
    o_ref[0] = acc.astype(BF16)


def _merge(o_a, o_d, o_c, h, w_a, w_d, w_c, w_gate):
    bsz, s, d = h.shape
    tm, tn = 1024, 256
    nd = d // tn

    def rows(width):
        return pl.BlockSpec((1, tm, width), lambda b, i, j: (b, i, 0))

    def wcols(kdim, off=0):
        return pl.BlockSpec((kdim, tn), lambda b, i, j: (0, off + j))

    return pl.pallas_call(
        _merge_kernel,
        grid=(bsz, s // tm, nd),
        in_specs=[rows(o_a.shape[2]), rows(o_d.shape[2]), rows(o_c.shape[2]), rows(d),
                  wcols(w_a.shape[0]), wcols(w_d.shape[0]), wcols(w_c.shape[0]),
                  wcols(d), wcols(d, nd), wcols(d, 2 * nd)],
        out_specs=pl.BlockSpec((1, tm, tn), lambda b, i, j: (b, i, j)),
        out_shape=jax.ShapeDtypeStruct((bsz, s, d), BF16),
        compiler_params=_params(("parallel", "parallel", "arbitrary")),
        name="branch_merge",
    )(o_a, o_d, o_c, h, w_a, w_d, w_c, w_gate, w_gate, w_gate)


_C_GDN = 3 * ATTN_WIDTH
_C_BA = _C_GDN + 3 * GDN_WIDTH
_C_Z = _C_BA + 2 * GDN_HEADS
_C_U = _C_Z + GDN_WIDTH
_C_GATE = _C_U + 2 * CONV_CH
_C_END = _C_GATE + 3 * D_MODEL


def _mixer(h, l, cosf, sinf, w_in, q_norm_g, k_norm_g, w_attn_o, gdn_conv_w, gdn_a_log, gdn_dt_bias,
           gdn_norm_g, w_gdn_o, conv_dw_w, conv_dw_b, conv_ln_g, conv_ln_b, w_conv_o):
    bsz, s, _ = h.shape
    w = w_in[l]
    w_qkv = w[:, :_C_GDN].astype(BF16)
    w_d = w[:, _C_GDN:_C_BA].astype(BF16)
    w_ba_t = w[:, _C_BA:_C_Z].T.astype(BF16)
    w_z = w[:, _C_Z:_C_U].astype(BF16)
    w_u = w[:, _C_U:_C_GATE].astype(BF16)
    w_gate = w[:, _C_GATE:_C_END].astype(BF16)

    gains = jnp.stack([q_norm_g[l] * (HEAD_DIM ** -0.5), k_norm_g[l],
                       jnp.ones((HEAD_DIM,), F32)]).reshape(3, 1, HEAD_DIM)
    qkv_groups = _proj_qkv(h, w_qkv, gains, cosf, sinf)
    outs, lses = zip(*[_attn_group(qkv_groups[g], g) for g in range(N_GROUPS)])
    o_a = _attn_merge(outs, lses)

    qkv_d = _mm(h, w_d, mode="plain", out_dtype=F32, tn=512)
    z = _mm(h, w_z, mode="silu", out_dtype=BF16, tn=512)
    sc = _gdn_scalars(h, w_ba_t, gdn_a_log[l], gdn_dt_bias[l])
    cols = jnp.swapaxes(sc[:, :5 * GDN_HEADS], 1, 2)
    grow = sc[:, 4 * GDN_HEADS:5 * GDN_HEADS]
    dcb = jnp.swapaxes(sc[:, 5 * GDN_HEADS:, ::GDN_CHUNK], 1, 2)
    dcb = jnp.broadcast_to(dcb[..., None], dcb.shape + (HEAD_DIM,))
    dcb = dcb.reshape(bsz, s // GDN_TILE, GDN_TILE // GDN_CHUNK, GDN_WIDTH)
    o_d = _gdn(qkv_d, gdn_conv_w[l], z, cols, grow, dcb, gdn_norm_g[l].reshape(1, HEAD_DIM))

    u = _mm(h, w_u, mode="glu", out_dtype=F32, tn=512, col1=CONV_CH, n_out=CONV_CH)
    dw = jnp.concatenate([conv_dw_w[l], jnp.zeros((1, CONV_CH), F32)], axis=0)
    o_c = _conformer(u, dw, conv_dw_b[l].reshape(1, CONV_CH), conv_ln_g[l].reshape(1, CONV_CH),
                     conv_ln_b[l].reshape(1, CONV_CH))

    return _merge(o_a, o_d, o_c, h, w_attn_o[l].astype(BF16), w_gdn_o[l].astype(BF16),
                  w_conv_o[l].astype(BF16), w_gate)


def kernel(x, c, positions, mix_mod_w, mix_mod_b, mix_norm_g, w_in, q_norm_g, k_norm_g, w_attn_o, gdn_conv_w, gdn_a_log, gdn_dt_bias, gdn_norm_g, w_gdn_o, conv_dw_w, conv_dw_b, conv_ln_g, conv_ln_b, w_conv_o, w_out, ffn_mod_w, ffn_mod_b, ffn_norm_g, w_gate_up, w_down):
    bsz, s, d = x.shape
    depth = w_in.shape[0]
    cosf, sinf = _rope_tables(positions)
    c_pad = jnp.concatenate([c, jnp.zeros((SUBLANES - bsz, d), c.dtype)], axis=0)
    mix_mods = _adaln_mods(c_pad, mix_mod_w, mix_mod_b)
    ffn_mods = _adaln_mods(c_pad, ffn_mod_w, ffn_mod_b)

    def mods(m, l):
        return tuple(m[l, :bsz, None, k * d:(k + 1) * d] for k in range(3))

    for l in range(depth):
        shift, scale, gate = mods(mix_mods, l)
        h = _mod_norm(x, mix_norm_g[l].reshape(1, d), scale, shift)
        m = _mixer(h, l, cosf, sinf, w_in, q_norm_g, k_norm_g, w_attn_o, gdn_conv_w, gdn_a_log,
                   gdn_dt_bias, gdn_norm_g, w_gdn_o, conv_dw_w, conv_dw_b, conv_ln_g, conv_ln_b, w_conv_o)
        x = _mm(m, w_out[l].astype(BF16), mode="residual", out_dtype=F32, tn=1024, x=x, gate=gate)
        shift, scale, gate = mods(ffn_mods, l)
        h = _mod_norm(x, ffn_norm_g[l].reshape(1, d), scale, shift)
        wgu = w_gate_up[l].astype(BF16)
        act = _mm(h, wgu, mode="swiglu", out_dtype=BF16, tn=512, col1=D_FF, n_out=D_FF)
        x = _mm(act, w_down[l].astype(BF16), mode="residual", out_dtype=F32, tn=512, x=x, gate=gate)
    return x
```

```python
import functools

import jax
import jax.numpy as jnp
from jax import lax
from jax.experimental import pallas as pl
from jax.experimental.pallas import tpu as pltpu

F32 = jnp.float32
BF16 = jnp.bfloat16

D_MODEL = 2048
HEAD_DIM = 128
ROPE_DIM = HEAD_DIM // 4
ROPE_HALF = ROPE_DIM // 2
ROPE_THETA = 500000.0
DILATIONS = (1, 4, 16)
ATTN_BLOCK = 128
ATTN_NQ = (4, 2, 1)
QKV_SUBTILES = 4
OUT_SUBTILES = 2
N_GROUPS = 3
GROUP_HEADS = 4
GROUP_WIDTH = GROUP_HEADS * HEAD_DIM
ATTN_WIDTH = N_GROUPS * GROUP_WIDTH
GDN_HEADS = 8
GDN_WIDTH = GDN_HEADS * HEAD_DIM
GDN_CHUNK = 256
GDN_TILE = GDN_CHUNK
GDN_NEUMANN_SQUARINGS = 7
GDN_LOCKSTEP = 4
GDN_PROJ_TN = 4 * HEAD_DIM
SHORT_CONV = 4
CONV_CH = D_MODEL // 2
CONV_K = 31
D_FF = 5632
NORM_EPS = 1e-6
NEG_BIG = -1e30
SUBLANES = 8

VMEM_LIMIT_V7X = 56 * 1024 * 1024


def _params(sem, vmem=VMEM_LIMIT_V7X):
    return pltpu.CompilerParams(dimension_semantics=sem, vmem_limit_bytes=vmem)


def _silu(t):
    return t * jax.nn.sigmoid(t)


def _dot(a, b):
    return jnp.dot(a, b, preferred_element_type=F32)


def _dot_nt(a, b):
    return lax.dot_general(a, b, (((1,), (1,)), ((), ())), preferred_element_type=F32)


def _dot_tn(a, b):
    return lax.dot_general(a, b, (((0,), (0,)), ((), ())), preferred_element_type=F32)


def _mod_kernel(c_ref, w_ref, b_ref, o_ref):
    c = c_ref[...]
    o_ref[0] = _dot(_silu(c).astype(BF16), w_ref[0].astype(BF16)) + b_ref[0]


def _adaln_mods(c_pad, w, b):
    n_layers, d, n = w.shape
    tn = 1024
    return pl.pallas_call(
        _mod_kernel,
        grid=(n_layers, n // tn),
        in_specs=[
            pl.BlockSpec((SUBLANES, d), lambda l, j: (0, 0)),
            pl.BlockSpec((1, d, tn), lambda l, j: (l, 0, j)),
            pl.BlockSpec((1, 1, tn), lambda l, j: (l, 0, j)),
        ],
        out_specs=pl.BlockSpec((1, SUBLANES, tn), lambda l, j: (l, 0, j)),
        out_shape=jax.ShapeDtypeStruct((n_layers, SUBLANES, n), F32),
        compiler_params=_params(("parallel", "parallel")),
        name="adaln_mods",
    )(c_pad, w, b.reshape(n_layers, 1, n))


def _norm_kernel(x_ref, g_ref, sc_ref, sh_ref, o_ref):
    x = x_ref[0]
    ms = jnp.mean(x * x, axis=-1, keepdims=True)
    y = x * lax.rsqrt(ms + NORM_EPS) * g_ref[...]
    o_ref[0] = (y * (1.0 + sc_ref[0]) + sh_ref[0]).astype(BF16)


def _mod_norm(x, g, scale, shift):
    bsz, s, d = x.shape
    tm = 512
    vec = pl.BlockSpec((1, 1, d), lambda b, i: (b, 0, 0))
    return pl.pallas_call(
        _norm_kernel,
        grid=(bsz, s // tm),
        in_specs=[
            pl.BlockSpec((1, tm, d), lambda b, i: (b, i, 0)),
            pl.BlockSpec((1, d), lambda b, i: (0, 0)),
            vec, vec,
        ],
        out_specs=pl.BlockSpec((1, tm, d), lambda b, i: (b, i, 0)),
        out_shape=jax.ShapeDtypeStruct((bsz, s, d), BF16),
        compiler_params=_params(("parallel", "parallel")),
        name="mod_norm",
    )(x, g, scale, shift)


def _rope_kernel(pos_ref, inv_ref, sgn_ref, cos_ref, sin_ref):
    ang = pos_ref[0] * inv_ref[...]
    cos_ref[0] = jnp.cos(ang)
    sin_ref[0] = jnp.sin(ang) * sgn_ref[...]


def _rope_tables(positions):
    bsz, s = positions.shape
    tm = 1024
    inv = ROPE_THETA ** (-jnp.arange(0, ROPE_DIM, 2, dtype=F32) / ROPE_DIM)
    zeros = jnp.zeros((HEAD_DIM - ROPE_DIM,), F32)
    inv_full = jnp.concatenate([inv, inv, zeros]).reshape(1, HEAD_DIM)
    sgn = jnp.concatenate([-jnp.ones((ROPE_HALF,), F32), jnp.ones((ROPE_HALF,), F32), zeros])
    pos = positions.astype(F32).reshape(bsz, s, 1)
    out = jax.ShapeDtypeStruct((bsz, s, HEAD_DIM), F32)
    row = pl.BlockSpec((1, HEAD_DIM), lambda b, i: (0, 0))
    blk = pl.BlockSpec((1, tm, HEAD_DIM), lambda b, i: (b, i, 0))
    return pl.pallas_call(
        _rope_kernel,
        grid=(bsz, s // tm),
        in_specs=[pl.BlockSpec((1, tm, 1), lambda b, i: (b, i, 0)), row, row],
        out_specs=[blk, blk],
        out_shape=[out, out],
        compiler_params=_params(("parallel", "parallel")),
        name="rope_tables",
    )(pos, inv_full, sgn.reshape(1, HEAD_DIM))


def _qkv_kernel(a_ref, w_ref, g_ref, cos_ref, sin_ref, o0_ref, o1_ref, o2_ref, acc_scr, *, tm):
    j = pl.program_id(2)
    kind = j % 3
    group = j // 3
    lane = lax.broadcasted_iota(jnp.int32, (1, HEAD_DIM), 1)
    sub = tm // QKV_SUBTILES

    def tile(o_ref, r, normed):
        def project(si):
            return _dot(a_ref[0, si * sub:(si + 1) * sub, :], w_ref[...])

        nxt = project(0)
        for si in range(QKV_SUBTILES):
            rows = slice(si * sub, (si + 1) * sub)
            acc = nxt
            if si + 1 < QKV_SUBTILES:
                nxt = project(si + 1)
            for hh in range(GROUP_HEADS):
                t = acc[:, hh * HEAD_DIM:(hh + 1) * HEAD_DIM]
                if normed:
                    ms = jnp.mean(t * t, axis=-1, keepdims=True)
                    t = t * lax.rsqrt(ms + NORM_EPS) * g_ref[0]
                    partner = jnp.where(lane < ROPE_HALF,
                                        pltpu.roll(t, HEAD_DIM - ROPE_HALF, 1),
                                        pltpu.roll(t, ROPE_HALF, 1))
                    t = t * cos_ref[0, rows, :] + partner * sin_ref[0, rows, :]
                if r == 1:
                    o_ref[0, 0, rows, hh * HEAD_DIM:(hh + 1) * HEAD_DIM] = t.astype(BF16)
                else:
                    acc_scr[hh, rows, :] = t
            if r > 1:
                orows = slice(si * sub // r, (si + 1) * sub // r)
                for rr in range(r):
                    for hh in range(GROUP_HEADS):
                        c0 = rr * GROUP_WIDTH + hh * HEAD_DIM
                        o_ref[0, 0, orows, c0:c0 + HEAD_DIM] = (
                            acc_scr[hh, pl.ds(si * sub + rr, sub // r, stride=r), :].astype(BF16))

    for gi, (o_ref, r) in enumerate(zip((o0_ref, o1_ref, o2_ref), DILATIONS)):
        for normed in (True, False):
            is_kind = (kind < 2) if normed else (kind == 2)
            pl.when(jnp.logical_and(group == gi, is_kind))(
                functools.partial(tile, o_ref, r, normed))


def _proj_qkv(h, w_qkv, gains, cosf, sinf):
    bsz, s, k = h.shape
    tm, tn = 1024, GROUP_WIDTH

    def out_spec(g, r):
        return pl.BlockSpec((1, 1, tm // r, r * tn),
                            lambda b, i, j: (jnp.clip(j - 3 * g, 0, 2), b, i, 0))

    return pl.pallas_call(
        functools.partial(_qkv_kernel, tm=tm),
        grid=(bsz, s // tm, 3 * N_GROUPS),
        in_specs=[
            pl.BlockSpec((1, tm, k), lambda b, i, j: (b, i, 0)),
            pl.BlockSpec((k, tn), lambda b, i, j: (0, (j % 3) * N_GROUPS + j // 3)),
            pl.BlockSpec((1, 1, HEAD_DIM), lambda b, i, j: (j % 3, 0, 0)),
            pl.BlockSpec((1, tm, HEAD_DIM), lambda b, i, j: (b, i, 0)),
            pl.BlockSpec((1, tm, HEAD_DIM), lambda b, i, j: (b, i, 0)),
        ],
        out_specs=[out_spec(g, r) for g, r in enumerate(DILATIONS)],
        out_shape=[jax.ShapeDtypeStruct((3, bsz, s // r, r * tn), BF16) for r in DILATIONS],
        scratch_shapes=[pltpu.VMEM((GROUP_HEADS, tm, HEAD_DIM), F32)],
        compiler_params=_params(("parallel", "parallel", "arbitrary")),
        name="proj_qkv",
    )(h, w_qkv, gains, cosf, sinf)


def _mm_kernel(a_ref, *refs, mode):
    o_ref = refs[-1]
    a = a_ref[0]
    if mode == "plain":
        o_ref[0] = _dot(a, refs[0][...]).astype(o_ref.dtype)
    elif mode == "silu":
        o_ref[0] = _silu(_dot(a, refs[0][...])).astype(o_ref.dtype)
    elif mode == "glu":
        o_ref[0] = (_dot(a, refs[0][...]) * jax.nn.sigmoid(_dot(a, refs[1][...]))).astype(o_ref.dtype)
    elif mode == "swiglu":
        o_ref[0] = (_silu(_dot(a, refs[0][...])) * _dot(a, refs[1][...])).astype(o_ref.dtype)
    elif mode == "residual":
        x_ref, gate_ref = refs[1], refs[2]
        o_ref[0] = x_ref[0] + gate_ref[0] * _dot(a, refs[0][...])
    else:
        raise ValueError(mode)


def _mm(a, w, *, mode, out_dtype, tn, tm=1024, col1=0, n_out=None, x=None, gate=None):
    bsz, s, k = a.shape
    n_out = w.shape[1] if n_out is None else n_out
    assert col1 % tn == 0 and n_out % tn == 0 and s % tm == 0
    j1 = col1 // tn
    in_specs = [pl.BlockSpec((1, tm, k), lambda b, i, j: (b, i, 0)),
                pl.BlockSpec((k, tn), lambda b, i, j: (0, j))]
    args = [a, w]
    if mode in ("glu", "swiglu"):
        in_specs.append(pl.BlockSpec((k, tn), lambda b, i, j: (0, j1 + j)))
        args.append(w)
    if mode == "residual":
        in_specs += [pl.BlockSpec((1, tm, tn), lambda b, i, j: (b, i, j)),
                     pl.BlockSpec((1, 1, tn), lambda b, i, j: (b, 0, j))]
        args += [x, gate]
    return pl.pallas_call(
        functools.partial(_mm_kernel, mode=mode),
        grid=(bsz, s // tm, n_out // tn),
        in_specs=in_specs,
        out_specs=pl.BlockSpec((1, tm, tn), lambda b, i, j: (b, i, j)),
        out_shape=jax.ShapeDtypeStruct((bsz, s, n_out), out_dtype),
        compiler_params=_params(("parallel", "parallel", "arbitrary")),
        name="mm_" + mode,
    )(*args)


def _gdn_proj_kernel(a_ref, w_ref, cw_ref, o_ref, slab, carry, *, tm):
    i = pl.program_id(1)
    j = pl.program_id(2)
    pad = SUBLANES
    sub = tm // QKV_SUBTILES
    heads_per_tile = GDN_PROJ_TN // HEAD_DIM

    @pl.when(i == 0)
    def _():
        slab[0:pad, :] = jnp.zeros((pad, GDN_PROJ_TN), F32)

    @pl.when(i > 0)
    def _():
        slab[0:pad, :] = carry[j]

    def project(si):
        r0 = si * sub
        acc = _dot(a_ref[0, r0:r0 + sub, :], w_ref[...])
        slab[pad + r0:pad + r0 + sub, :] = acc
        return acc

    def tile(normed, out_scale):
        nxt = project(0)
        for si in range(QKV_SUBTILES):
            r0 = si * sub
            acc = nxt
            if si + 1 < QKV_SUBTILES:
                nxt = project(si + 1)
            for hh in range(heads_per_tile):
                cols = slice(hh * HEAD_DIM, (hh + 1) * HEAD_DIM)
                y = cw_ref[SHORT_CONV - 1:SHORT_CONV, cols] * acc[:, cols]
                for kk in reversed(range(SHORT_CONV - 1)):
                    y = y + cw_ref[kk:kk + 1, cols] * slab[pl.ds(pad + r0 - SHORT_CONV + 1 + kk, sub), cols]
                y = _silu(y)
                if normed:
                    y = y * (lax.rsqrt(jnp.sum(y * y, axis=-1, keepdims=True) + NORM_EPS) * out_scale)
                o_ref[0, r0:r0 + sub, cols] = y.astype(BF16)
        carry[j] = slab[tm:tm + pad, :]

    q_tiles = GDN_WIDTH // GDN_PROJ_TN
    pl.when(j < q_tiles)(functools.partial(tile, True, HEAD_DIM ** -0.5))
    pl.when(jnp.logical_and(j >= q_tiles, j < 2 * q_tiles))(functools.partial(tile, True, 1.0))
    pl.when(j >= 2 * q_tiles)(functools.partial(tile, False, 1.0))


def _gdn_proj(h, w_d, conv_w):
    bsz, s, k = h.shape
    n = w_d.shape[1]
    tm, tn = 1024, GDN_PROJ_TN
    return pl.pallas_call(
        functools.partial(_gdn_proj_kernel, tm=tm),
        grid=(bsz, s // tm, n // tn),
        in_specs=[pl.BlockSpec((1, tm, k), lambda b, i, j: (b, i, 0)),
                  pl.BlockSpec((k, tn), lambda b, i, j: (0, j)),
                  pl.BlockSpec((SHORT_CONV, tn), lambda b, i, j: (0, j))],
        out_specs=pl.BlockSpec((1, tm, tn), lambda b, i, j: (b, i, j)),
        out_shape=jax.ShapeDtypeStruct((bsz, s, n), BF16),
        scratch_shapes=[pltpu.VMEM((tm + SUBLANES, tn), F32),
                        pltpu.VMEM((n // tn, SUBLANES, tn), F32)],
        compiler_params=_params(("parallel", "arbitrary", "arbitrary")),
        name="gdn_proj",
    )(h, w_d, conv_w)


def _attn_kernel(q_ref, kc_ref, kp_ref, vc_ref, vp_ref, o_ref, l_ref, o_scr, *, r, nq):
    n = pl.program_id(1)
    qi = lax.broadcasted_iota(jnp.int32, (ATTN_BLOCK, ATTN_BLOCK), 0)
    kj = lax.broadcasted_iota(jnp.int32, (ATTN_BLOCK, ATTN_BLOCK), 1)
    cur_ok = kj <= qi
    prev_band = kj >= qi
    prev_first = jnp.logical_and(prev_band, n > 0)
    for rr in range(r):
        for t in range(nq):
            rows = slice(t * ATTN_BLOCK, (t + 1) * ATTN_BLOCK)
            prows = slice((t - 1) * ATTN_BLOCK, t * ATTN_BLOCK)
            prev_ok = prev_first if t == 0 else prev_band
            heads = range(GROUP_HEADS)
            cols = [slice(rr * GROUP_WIDTH + hh * HEAD_DIM, rr * GROUP_WIDTH + (hh + 1) * HEAD_DIM)
                    for hh in heads]
            if t == 0:
                kps = [kp_ref[0, 0, :, cols[hh]] for hh in heads]
                vps = [vp_ref[0, 0, :, cols[hh]] for hh in heads]
            else:
                kps = [kc_ref[0, 0, prows, cols[hh]] for hh in heads]
                vps = [vc_ref[0, 0, prows, cols[hh]] for hh in heads]
            qs = [q_ref[0, 0, rows, cols[hh]] for hh in heads]
            s_c = [jnp.where(cur_ok, _dot_nt(qs[hh], kc_ref[0, 0, rows, cols[hh]]), NEG_BIG) for hh in heads]
            s_p = [jnp.where(prev_ok, _dot_nt(qs[hh], kps[hh]), NEG_BIG) for hh in heads]
            m = [jnp.maximum(jnp.max(s_c[hh], axis=-1, keepdims=True),
                             jnp.max(s_p[hh], axis=-1, keepdims=True)) for hh in heads]
            p_c = [jnp.exp(s_c[hh] - m[hh]) for hh in heads]
            p_p = [jnp.exp(s_p[hh] - m[hh]) for hh in heads]
            den = [jnp.sum(p_c[hh], axis=-1, keepdims=True) + jnp.sum(p_p[hh], axis=-1, keepdims=True)
                   for hh in heads]
            acc = [_dot(p_c[hh].astype(BF16), vc_ref[0, 0, rows, cols[hh]])
                   + _dot(p_p[hh].astype(BF16), vps[hh]) for hh in heads]
            for hh in heads:
                o = acc[hh] / den[hh]
                if r == 1:
                    o_scr[hh, rows, :] = o
                else:
                    o_scr[hh, pl.ds(t * ATTN_BLOCK * r + rr, ATTN_BLOCK, stride=r), :] = o
                lcol = rr * GROUP_HEADS + hh
                l_ref[0, rows, lcol:lcol + 1] = m[hh] + jnp.log(den[hh])
    for hh in range(GROUP_HEADS):
        o_ref[0, :, hh * HEAD_DIM:(hh + 1) * HEAD_DIM] = o_scr[hh].astype(BF16)


def _attn_group(qkv, g):
    _, bsz, l, rw = qkv.shape
    r, nq = DILATIONS[g], ATTN_NQ[g]
    tq = ATTN_BLOCK * nq
    tp = tq * r
    s = l * r

    def cur(kind):
        return pl.BlockSpec((1, 1, tq, rw), lambda b, n: (kind, b, n, 0))

    def prev(kind):
        return pl.BlockSpec((1, 1, ATTN_BLOCK, rw), lambda b, n: (kind, b, jnp.maximum(n * nq - 1, 0), 0))

    o, lse = pl.pallas_call(
        functools.partial(_attn_kernel, r=r, nq=nq),
        grid=(bsz, l // tq),
        in_specs=[cur(0), cur(1), prev(1), cur(2), prev(2)],
        out_specs=[pl.BlockSpec((1, tp, GROUP_WIDTH), lambda b, n: (b, n, 0)),
                   pl.BlockSpec((1, tq, r * GROUP_HEADS), lambda b, n: (b, n, 0))],
        out_shape=[jax.ShapeDtypeStruct((bsz, s, GROUP_WIDTH), BF16),
                   jax.ShapeDtypeStruct((bsz, l, r * GROUP_HEADS), F32)],
        scratch_shapes=[pltpu.VMEM((GROUP_HEADS, tp, HEAD_DIM), F32)],
        compiler_params=_params(("parallel", "arbitrary")),
        name=f"attn_g{g}",
    )(qkv, qkv, qkv, qkv, qkv)
    return o, lse.reshape(bsz, s, GROUP_HEADS)


def _attn_merge_kernel(o0, o1, o2, l0, l1, l2, out_ref):
    a0, a1, a2 = l0[0], l1[0], l2[0]
    m = jnp.maximum(jnp.maximum(a0, a1), a2)
    e0, e1, e2 = jnp.exp(a0 - m), jnp.exp(a1 - m), jnp.exp(a2 - m)
    inv = 1.0 / (e0 + e1 + e2)
    w0, w1, w2 = e0 * inv, e1 * inv, e2 * inv
    for hh in range(GROUP_HEADS):
        cols = slice(hh * HEAD_DIM, (hh + 1) * HEAD_DIM)
        hc = slice(hh, hh + 1)
        acc = w0[:, hc] * o0[0, :, cols].astype(F32)
        acc = acc + w1[:, hc] * o1[0, :, cols].astype(F32)
        acc = acc + w2[:, hc] * o2[0, :, cols].astype(F32)
        out_ref[0, :, cols] = acc.astype(BF16)


def _attn_merge(outs, lses):
    bsz, s, w = outs[0].shape
    tm = 1024
    blk = pl.BlockSpec((1, tm, w), lambda b, i: (b, i, 0))
    lblk = pl.BlockSpec((1, tm, GROUP_HEADS), lambda b, i: (b, i, 0))
    return pl.pallas_call(
        _attn_merge_kernel,
        grid=(bsz, s // tm),
        in_specs=[blk] * 3 + [lblk] * 3,
        out_specs=blk,
        out_shape=jax.ShapeDtypeStruct((bsz, s, w), BF16),
        compiler_params=_params(("parallel", "parallel")),
        name="attn_merge",
    )(*outs, *lses)


def _conformer_kernel(h_ref, wu_ref, w_ref, b_ref, g_ref, beta_ref, o_ref, scr, sh_scr, y_scr, *, tm):
    i = pl.program_id(1)
    pad = 32

    @pl.when(i == 0)
    def _():
        scr[0:pad, :] = jnp.zeros((pad, CONV_CH), F32)

    @pl.when(i > 0)
    def _():
        scr[0:pad, :] = scr[tm:tm + pad, :]

    h = h_ref[0]
    scr[pad:pad + tm, :] = (_dot(h, wu_ref[:, :CONV_CH])
                            * jax.nn.sigmoid(_dot(h, wu_ref[:, CONV_CH:])))
    for b in range(1, SUBLANES):
        sh_scr[b - 1] = scr[pl.ds(b, tm + pad - SUBLANES), :]
    for c in range(CONV_CH // HEAD_DIM):
        cols = slice(c * HEAD_DIM, (c + 1) * HEAD_DIM)
        acc = None
        for kk in range(CONV_K):
            a8, b = divmod(pad - CONV_K + 1 + kk, SUBLANES)
            src = scr if b == 0 else sh_scr.at[b - 1]
            term = w_ref[kk:kk + 1, cols] * src[pl.ds(a8 * SUBLANES, tm), cols]
            acc = term if acc is None else acc + term
        y_scr[:, cols] = acc + b_ref[:, cols]
    y = y_scr[...]
    mu = jnp.mean(y, axis=-1, keepdims=True)
    yc = y - mu
    var = jnp.mean(yc * yc, axis=-1, keepdims=True)
    t = yc * lax.rsqrt(var + NORM_EPS) * g_ref[...] + beta_ref[...]
    o_ref[0] = _silu(t).astype(BF16)


def _conformer(h, w_u, w, b, g, beta):
    bsz, s, d = h.shape
    n = CONV_CH
    tm = 256
    row = pl.BlockSpec((1, n), lambda bb, i: (0, 0))
    return pl.pallas_call(
        functools.partial(_conformer_kernel, tm=tm),
        grid=(bsz, s // tm),
        in_specs=[
            pl.BlockSpec((1, tm, d), lambda bb, i: (bb, i, 0)),
            pl.BlockSpec((d, 2 * n), lambda bb, i: (0, 0)),
            pl.BlockSpec((32, n), lambda bb, i: (0, 0)),
            row, row, row,
        ],
        out_specs=pl.BlockSpec((1, tm, n), lambda bb, i: (bb, i, 0)),
        out_shape=jax.ShapeDtypeStruct((bsz, s, n), BF16),
        scratch_shapes=[pltpu.VMEM((tm + 32, n), F32),
                        pltpu.VMEM((SUBLANES - 1, tm + 32 - SUBLANES, n), F32),
                        pltpu.VMEM((tm, n), F32)],
        compiler_params=_params(("parallel", "arbitrary")),
        name="conformer_conv",
    )(h, w_u, w, b, g, beta)


def _gdn_scalar_kernel(h_ref, w_ref, alog_ref, dtb_ref, o_ref, *, ts):
    x = _dot_nt(w_ref[...], h_ref[0])
    beta = jax.nn.sigmoid(x[0:GDN_HEADS])
    g = -jnp.exp(alog_ref[...]) * jax.nn.softplus(x[GDN_HEADS:2 * GDN_HEADS] + dtb_ref[...])
    lane = lax.broadcasted_iota(jnp.int32, (GDN_HEADS, ts), 1) % GDN_CHUNK
    fwd, bwd = g, g
    sh = 1
    while sh < GDN_CHUNK:
        fwd = fwd + jnp.where(lane >= sh, pltpu.roll(fwd, sh, 1), 0.0)
        bwd = bwd + jnp.where(lane < GDN_CHUNK - sh, pltpu.roll(bwd, ts - sh, 1), 0.0)
        sh *= 2
    rest = bwd - g
    eg = jnp.exp(fwd)
    o_ref[0, 0 * GDN_HEADS:1 * GDN_HEADS] = beta
    o_ref[0, 1 * GDN_HEADS:2 * GDN_HEADS] = eg
    o_ref[0, 2 * GDN_HEADS:3 * GDN_HEADS] = beta * eg
    o_ref[0, 3 * GDN_HEADS:4 * GDN_HEADS] = jnp.exp(rest)
    o_ref[0, 4 * GDN_HEADS:5 * GDN_HEADS] = fwd
    o_ref[0, 5 * GDN_HEADS:6 * GDN_HEADS] = jnp.exp(fwd + rest)


def _gdn_scalars(h, w_ba_t, a_log, dt_bias):
    bsz, s, d = h.shape
    ts = 2048
    col = pl.BlockSpec((GDN_HEADS, 1), lambda b, i: (0, 0))
    return pl.pallas_call(
        functools.partial(_gdn_scalar_kernel, ts=ts),
        grid=(bsz, s // ts),
        in_specs=[pl.BlockSpec((1, ts, d), lambda b, i: (b, i, 0)),
                  pl.BlockSpec((2 * GDN_HEADS, d), lambda b, i: (0, 0)), col, col],
        out_specs=pl.BlockSpec((1, 6 * GDN_HEADS, ts), lambda b, i: (b, 0, i)),
        out_shape=jax.ShapeDtypeStruct((bsz, 6 * GDN_HEADS, s), F32),
        compiler_params=_params(("parallel", "parallel")),
        name="gdn_scalars",
    )(h, w_ba_t, a_log.reshape(GDN_HEADS, 1), dt_bias.reshape(GDN_HEADS, 1))


def _gdn_kernel(x_ref, z_ref, col_ref, grow_ref, dc_ref, ng_ref, o_ref, state):
    i = pl.program_id(1)
    ts = GDN_TILE

    @pl.when(i == 0)
    def _():
        state[...] = jnp.zeros_like(state)

    ri = lax.broadcasted_iota(jnp.int32, (ts, ts), 0)
    ci = lax.broadcasted_iota(jnp.int32, (ts, ts), 1)
    causal = ri >= ci
    strict = ri > ci

    def head_cols(part, h):
        c0 = part * GDN_WIDTH + h * HEAD_DIM
        return slice(c0, c0 + HEAD_DIM)

    for h0 in range(0, GDN_HEADS, GDN_LOCKSTEP):
        heads = range(h0, h0 + GDN_LOCKSTEP)
        pre = {}
        for h in heads:
            q = x_ref[0, :, head_cols(0, h)].astype(F32)
            k = x_ref[0, :, head_cols(1, h)].astype(F32)
            v = x_ref[0, :, head_cols(2, h)].astype(F32)
            beta, gam, bgam, kdec, gcol = (col_ref[0, :, c * GDN_HEADS + h:c * GDN_HEADS + h + 1]
                                           for c in range(5))
            grow = grow_ref[0, h:h + 1, :]
            ktb = k.T.astype(BF16)
            kb = k.astype(BF16)
            decay = jnp.exp(jnp.where(causal, gcol - grow, NEG_BIG))
            a = jnp.where(strict, beta * _dot(kb, ktb) * decay, 0.0)
            qkb = (_dot(q.astype(BF16), ktb) * decay).astype(BF16)
            sol = jnp.concatenate([v * beta, k * bgam], axis=1)
            pre[h] = dict(p=a.astype(BF16), sol=sol, qkb=qkb,
                          qd=(q * gam).astype(BF16), kdt=(k * kdec).T.astype(BF16))

        for h in heads:
            d = pre[h]
            d["sol"] = d["sol"] - _dot(d["p"], d["sol"].astype(BF16))
        for _ in range(GDN_NEUMANN_SQUARINGS):
            for h in heads:
                d = pre[h]
                d["p"] = _dot(d["p"], d["p"]).astype(BF16)
            for h in heads:
                d = pre[h]
                d["sol"] = d["sol"] + _dot(d["p"], d["sol"].astype(BF16))

        for h in heads:
            d = pre[h]
            hcols = slice(h * HEAD_DIM, (h + 1) * HEAD_DIM)
            st = state[h]
            sb = st.astype(BF16)
            v_new = d["sol"][:, :HEAD_DIM] - _dot(d["sol"][:, HEAD_DIM:].astype(BF16), sb)
            vb = v_new.astype(BF16)
            o = _dot(d["qd"], sb) + _dot(d["qkb"], vb)
            state[h] = st * dc_ref[0, 0, :, hcols] + _dot(d["kdt"], vb)
            o = o * lax.rsqrt(jnp.mean(o * o, axis=-1, keepdims=True) + NORM_EPS) * ng_ref[...]
            o_ref[0, :, hcols] = (o * z_ref[0, :, hcols].astype(F32)).astype(BF16)


def _gdn(qkv_d, z, cols, grow, dcb, norm_g):
    bsz, s, n = qkv_d.shape
    ts = GDN_TILE
    return pl.pallas_call(
        _gdn_kernel,
        grid=(bsz, s // ts),
        in_specs=[
            pl.BlockSpec((1, ts, n), lambda b, i: (b, i, 0)),
            pl.BlockSpec((1, ts, GDN_WIDTH), lambda b, i: (b, i, 0)),
            pl.BlockSpec((1, ts, 5 * GDN_HEADS), lambda b, i: (b, i, 0)),
            pl.BlockSpec((1, GDN_HEADS, ts), lambda b, i: (b, 0, i)),
            pl.BlockSpec((1, 1, ts // GDN_CHUNK, GDN_WIDTH), lambda b, i: (b, i, 0, 0)),
            pl.BlockSpec((1, HEAD_DIM), lambda b, i: (0, 0)),
        ],
        out_specs=pl.BlockSpec((1, ts, GDN_WIDTH), lambda b, i: (b, i, 0)),
        out_shape=jax.ShapeDtypeStruct((bsz, s, GDN_WIDTH), BF16),
        scratch_shapes=[pltpu.VMEM((GDN_HEADS, HEAD_DIM, HEAD_DIM), F32)],
        compiler_params=_params(("parallel", "arbitrary")),
        name="gdn_delta",
    )(qkv_d, z, cols, grow, dcb, norm_g)


def _merge_kernel(oa_ref, od_ref, oc_ref, h_ref, wa_ref, wd_ref, wc_ref, ga_ref, gb_ref, gc_ref, o_ref):
    h = h_ref[0]
    acc = jax.nn.sigmoid(_dot(h, ga_ref[...])) * _dot(oa_ref[0], wa_ref[...])
    acc = acc + jax.nn.sigmoid(_dot(h, gb_ref[...])) * _dot(od_ref[0], wd_ref[...])
    acc = acc + jax.nn.sigmoid(_dot(h, gc_ref[...])) * _dot(oc_ref[0], wc_ref[...])
    o_ref[0] = acc.astype(BF16)


def _merge(o_a, o_d, o_c, h, w_a, w_d, w_c, w_gate):
    bsz, s, d = h.shape
    tm, tn = 1024, 256
    nd = d // tn

    def rows(width):
        return pl.BlockSpec((1, tm, width), lambda b, i, j: (b, i, 0))

    def wcols(kdim, off=0):
        return pl.BlockSpec((kdim, tn), lambda b, i, j: (0, off + j))

    return pl.pallas_call(
        _merge_kernel,
        grid=(bsz, s // tm, nd),
        in_specs=[rows(o_a.shape[2]), rows(o_d.shape[2]), rows(o_c.shape[2]), rows(d),
                  wcols(w_a.shape[0]), wcols(w_d.shape[0]), wcols(w_c.shape[0]),
                  wcols(d), wcols(d, nd), wcols(d, 2 * nd)],
        out_specs=pl.BlockSpec((1, tm, tn), lambda b, i, j: (b, i, j)),
        out_shape=jax.ShapeDtypeStruct((bsz, s, d), BF16),
        compiler_params=_params(("parallel", "parallel", "arbitrary")),
        name="branch_merge",
    )(o_a, o_d, o_c, h, w_a, w_d, w_c, w_gate, w_gate, w_gate)


def _out_norm_kernel(m_ref, w_ref, x_ref, gate_ref, g_ref, sc_ref, sh_ref, x1_ref, h_ref, *, tm):
    sub = tm // OUT_SUBTILES
    for si in range(OUT_SUBTILES):
        rows = slice(si * sub, (si + 1) * sub)
        x1 = x_ref[0, rows, :] + gate_ref[0] * _dot(m_ref[0, rows, :], w_ref[...])
        x1_ref[0, rows, :] = x1
        ms = jnp.mean(x1 * x1, axis=-1, keepdims=True)
        y = x1 * lax.rsqrt(ms + NORM_EPS) * g_ref[...]
        h_ref[0, rows, :] = (y * (1.0 + sc_ref[0]) + sh_ref[0]).astype(BF16)


def _out_norm(m, w, x, gate, g, scale, shift):
    bsz, s, d = x.shape
    tm = 512
    vec = pl.BlockSpec((1, 1, d), lambda b, i: (b, 0, 0))
    blk = pl.BlockSpec((1, tm, d), lambda b, i: (b, i, 0))
    return pl.pallas_call(
        functools.partial(_out_norm_kernel, tm=tm),
        grid=(bsz, s // tm),
        in_specs=[blk, pl.BlockSpec((d, d), lambda b, i: (0, 0)), blk, vec,
                  pl.BlockSpec((1, d), lambda b, i: (0, 0)), vec, vec],
        out_specs=[blk, blk],
        out_shape=[jax.ShapeDtypeStruct((bsz, s, d), F32), jax.ShapeDtypeStruct((bsz, s, d), BF16)],
        compiler_params=_params(("parallel", "parallel")),
        name="out_proj_norm",
    )(m, w, x, gate, g, scale, shift)


_C_GDN = 3 * ATTN_WIDTH
_C_BA = _C_GDN + 3 * GDN_WIDTH
_C_Z = _C_BA + 2 * GDN_HEADS
_C_U = _C_Z + GDN_WIDTH
_C_GATE = _C_U + 2 * CONV_CH
_C_END = _C_GATE + 3 * D_MODEL


def _mixer(h, l, cosf, sinf, w_in, q_norm_g, k_norm_g, w_attn_o, gdn_conv_w, gdn_a_log, gdn_dt_bias,
           gdn_norm_g, w_gdn_o, conv_dw_w, conv_dw_b, conv_ln_g, conv_ln_b, w_conv_o):
    bsz, s, _ = h.shape
    w = w_in[l]
    w_qkv = w[:, :_C_GDN].astype(BF16)
    w_d = w[:, _C_GDN:_C_BA].astype(BF16)
    w_ba_t = w[:, _C_BA:_C_Z].T.astype(BF16)
    w_z = w[:, _C_Z:_C_U].astype(BF16)
    w_u = w[:, _C_U:_C_GATE].astype(BF16)
    w_gate = w[:, _C_GATE:_C_END].astype(BF16)

    gains = jnp.stack([q_norm_g[l] * (HEAD_DIM ** -0.5), k_norm_g[l],
                       jnp.ones((HEAD_DIM,), F32)]).reshape(3, 1, HEAD_DIM)
    qkv_groups = _proj_qkv(h, w_qkv, gains, cosf, sinf)
    outs, lses = zip(*[_attn_group(qkv_groups[g], g) for g in range(N_GROUPS)])
    o_a = _attn_merge(outs, lses)

    qkv_d = _gdn_proj(h, w_d, gdn_conv_w[l])
    z = _mm(h, w_z, mode="silu", out_dtype=BF16, tn=512)
    sc = _gdn_scalars(h, w_ba_t, gdn_a_log[l], gdn_dt_bias[l])
    cols = jnp.swapaxes(sc[:, :5 * GDN_HEADS], 1, 2)
    grow = sc[:, 4 * GDN_HEADS:5 * GDN_HEADS]
    dcb = jnp.swapaxes(sc[:, 5 * GDN_HEADS:, ::GDN_CHUNK], 1, 2)
    dcb = jnp.broadcast_to(dcb[..., None], dcb.shape + (HEAD_DIM,))
    dcb = dcb.reshape(bsz, s // GDN_TILE, GDN_TILE // GDN_CHUNK, GDN_WIDTH)
    o_d = _gdn(qkv_d, z, cols, grow, dcb, gdn_norm_g[l].reshape(1, HEAD_DIM))

    dw = jnp.concatenate([conv_dw_w[l], jnp.zeros((1, CONV_CH), F32)], axis=0)
    o_c = _conformer(h, w_u, dw, conv_dw_b[l].reshape(1, CONV_CH), conv_ln_g[l].reshape(1, CONV_CH),
                     conv_ln_b[l].reshape(1, CONV_CH))

    return _merge(o_a, o_d, o_c, h, w_attn_o[l].astype(BF16), w_gdn_o[l].astype(BF16),
                  w_conv_o[l].astype(BF16), w_gate)


def kernel(x, c, positions, mix_mod_w, mix_mod_b, mix_norm_g, w_in, q_norm_g, k_norm_g, w_attn_o, gdn_conv_w, gdn_a_log, gdn_dt_bias, gdn_norm_g, w_gdn_o, conv_dw_w, conv_dw_b, conv_ln_g, conv_ln_b, w_conv_o, w_out, ffn_mod_w, ffn_mod_b, ffn_norm_g, w_gate_up, w_down):
    bsz, s, d = x.shape
    depth = w_in.shape[0]
    cosf, sinf = _rope_tables(positions)
    c_pad = jnp.concatenate([c, jnp.zeros((SUBLANES - bsz, d), c.dtype)], axis=0)
    mix_mods = _adaln_mods(c_pad, mix_mod_w, mix_mod_b)
    ffn_mods = _adaln_mods(c_pad, ffn_mod_w, ffn_mod_b)

    def mods(m, l):
        return tuple(m[l, :bsz, None, k * d:(k + 1) * d] for k in range(3))

    for l in range(depth):
        shift, scale, gate = mods(mix_mods, l)
        h = _mod_norm(x, mix_norm_g[l].reshape(1, d), scale, shift)
        m = _mixer(h, l, cosf, sinf, w_in, q_norm_g, k_norm_g, w_attn_o, gdn_conv_w, gdn_a_log,
                   gdn_dt_bias, gdn_norm_g, w_gdn_o, conv_dw_w, conv_dw_b, conv_ln_g, conv_ln_b, w_conv_o)
        shift2, scale2, gate2 = mods(ffn_mods, l)
        x, h = _out_norm(m, w_out[l].astype(BF16), x, gate, ffn_norm_g[l].reshape(1, d), scale2, shift2)
        gate = gate2
        wgu = w_gate_up[l].astype(BF16)
        act = _mm(h, wgu, mode="swiglu", out_dtype=BF16, tn=512, tm=2048, col1=D_FF, n_out=D_FF)
        x = _mm(act, w_down[l].astype(BF16), mode="residual", out_dtype=F32, tn=512, x=x, gate=gate)
    return x
```

```python
import functools

import jax
import jax.numpy as jnp
from jax import lax
from jax.experimental import pallas as pl
from jax.experimental.pallas import tpu as pltpu

F32 = jnp.float32
BF16 = jnp.bfloat16

D_MODEL = 2048
HEAD_DIM = 128
ROPE_DIM = HEAD_DIM // 4
ROPE_HALF = ROPE_DIM // 2
ROPE_THETA = 500000.0
DILATIONS = (1, 4, 16)
ATTN_BLOCK = 128
ATTN_NQ = (4, 2, 1)
QKV_SUBTILES = 8
OUT_SUBTILES = 2
N_GROUPS = 3
GROUP_HEADS = 4
GROUP_WIDTH = GROUP_HEADS * HEAD_DIM
ATTN_WIDTH = N_GROUPS * GROUP_WIDTH
GDN_HEADS = 8
GDN_WIDTH = GDN_HEADS * HEAD_DIM
GDN_CHUNK = 256
GDN_TILE = GDN_CHUNK
GDN_NEUMANN_SQUARINGS = 7
GDN_LOCKSTEP = 4
GDN_PROJ_TN = 4 * HEAD_DIM
SHORT_CONV = 4
CONV_CH = D_MODEL // 2
CONV_K = 31
D_FF = 5632
NORM_EPS = 1e-6
NEG_BIG = -1e30
SUBLANES = 8

VMEM_LIMIT_V7X = 56 * 1024 * 1024


def _params(sem, vmem=VMEM_LIMIT_V7X):
    return pltpu.CompilerParams(dimension_semantics=sem, vmem_limit_bytes=vmem)


def _silu(t):
    return t * jax.nn.sigmoid(t)


def _dot(a, b):
    return jnp.dot(a, b, preferred_element_type=F32)


def _dot_nt(a, b):
    return lax.dot_general(a, b, (((1,), (1,)), ((), ())), preferred_element_type=F32)


def _dot_tn(a, b):
    return lax.dot_general(a, b, (((0,), (0,)), ((), ())), preferred_element_type=F32)


def _mod_kernel(c_ref, w_ref, b_ref, o_ref):
    c = c_ref[...]
    o_ref[0] = _dot(_silu(c).astype(BF16), w_ref[0].astype(BF16)) + b_ref[0]


def _adaln_mods(c_pad, w, b):
    n_layers, d, n = w.shape
    tn = 1024
    return pl.pallas_call(
        _mod_kernel,
        grid=(n_layers, n // tn),
        in_specs=[
            pl.BlockSpec((SUBLANES, d), lambda l, j: (0, 0)),
            pl.BlockSpec((1, d, tn), lambda l, j: (l, 0, j)),
            pl.BlockSpec((1, 1, tn), lambda l, j: (l, 0, j)),
        ],
        out_specs=pl.BlockSpec((1, SUBLANES, tn), lambda l, j: (l, 0, j)),
        out_shape=jax.ShapeDtypeStruct((n_layers, SUBLANES, n), F32),
        compiler_params=_params(("parallel", "parallel")),
        name="adaln_mods",
    )(c_pad, w, b.reshape(n_layers, 1, n))


def _norm_kernel(x_ref, g_ref, sc_ref, sh_ref, o_ref):
    x = x_ref[0]
    ms = jnp.mean(x * x, axis=-1, keepdims=True)
    y = x * lax.rsqrt(ms + NORM_EPS) * g_ref[...]
    o_ref[0] = (y * (1.0 + sc_ref[0]) + sh_ref[0]).astype(BF16)


def _mod_norm(x, g, scale, shift):
    bsz, s, d = x.shape
    tm = 512
    vec = pl.BlockSpec((1, 1, d), lambda b, i: (b, 0, 0))
    return pl.pallas_call(
        _norm_kernel,
        grid=(bsz, s // tm),
        in_specs=[
            pl.BlockSpec((1, tm, d), lambda b, i: (b, i, 0)),
            pl.BlockSpec((1, d), lambda b, i: (0, 0)),
            vec, vec,
        ],
        out_specs=pl.BlockSpec((1, tm, d), lambda b, i: (b, i, 0)),
        out_shape=jax.ShapeDtypeStruct((bsz, s, d), BF16),
        compiler_params=_params(("parallel", "parallel")),
        name="mod_norm",
    )(x, g, scale, shift)


def _rope_kernel(pos_ref, inv_ref, sgn_ref, cos_ref, sin_ref):
    ang = pos_ref[0] * inv_ref[...]
    cos_ref[0] = jnp.cos(ang)
    sin_ref[0] = jnp.sin(ang) * sgn_ref[...]


def _rope_tables(positions):
    bsz, s = positions.shape
    tm = 1024
    inv = ROPE_THETA ** (-jnp.arange(0, ROPE_DIM, 2, dtype=F32) / ROPE_DIM)
    zeros = jnp.zeros((HEAD_DIM - ROPE_DIM,), F32)
    inv_full = jnp.concatenate([inv, inv, zeros]).reshape(1, HEAD_DIM)
    sgn = jnp.concatenate([-jnp.ones((ROPE_HALF,), F32), jnp.ones((ROPE_HALF,), F32), zeros])
    pos = positions.astype(F32).reshape(bsz, s, 1)
    out = jax.ShapeDtypeStruct((bsz, s, HEAD_DIM), F32)
    row = pl.BlockSpec((1, HEAD_DIM), lambda b, i: (0, 0))
    blk = pl.BlockSpec((1, tm, HEAD_DIM), lambda b, i: (b, i, 0))
    return pl.pallas_call(
        _rope_kernel,
        grid=(bsz, s // tm),
        in_specs=[pl.BlockSpec((1, tm, 1), lambda b, i: (b, i, 0)), row, row],
        out_specs=[blk, blk],
        out_shape=[out, out],
        compiler_params=_params(("parallel", "parallel")),
        name="rope_tables",
    )(pos, inv_full, sgn.reshape(1, HEAD_DIM))


def _qkv_kernel(a_ref, w_ref, g_ref, cos_ref, sin_ref, o0_ref, o1_ref, o2_ref, acc_scr, *, tm):
    j = pl.program_id(2)
    kind = j % 3
    group = j // 3
    lane = lax.broadcasted_iota(jnp.int32, (1, HEAD_DIM), 1)
    sub = tm // QKV_SUBTILES

    def tile(o_ref, r, normed):
        def project(si):
            return _dot(a_ref[0, si * sub:(si + 1) * sub, :], w_ref[...])

        nxt = project(0)
        for si in range(QKV_SUBTILES):
            rows = slice(si * sub, (si + 1) * sub)
            acc = nxt
            if si + 1 < QKV_SUBTILES:
                nxt = project(si + 1)
            for hh in range(GROUP_HEADS):
                t = acc[:, hh * HEAD_DIM:(hh + 1) * HEAD_DIM]
                if normed:
                    ms = jnp.mean(t * t, axis=-1, keepdims=True)
                    t = t * lax.rsqrt(ms + NORM_EPS) * g_ref[0]
                    partner = jnp.where(lane < ROPE_HALF,
                                        pltpu.roll(t, HEAD_DIM - ROPE_HALF, 1),
                                        pltpu.roll(t, ROPE_HALF, 1))
                    t = t * cos_ref[0, rows, :] + partner * sin_ref[0, rows, :]
                if r == 1:
                    o_ref[0, 0, rows, hh * HEAD_DIM:(hh + 1) * HEAD_DIM] = t.astype(BF16)
                else:
                    acc_scr[hh, rows, :] = t
            if r > 1:
                orows = slice(si * sub // r, (si + 1) * sub // r)
                for rr in range(r):
                    for hh in range(GROUP_HEADS):
                        c0 = rr * GROUP_WIDTH + hh * HEAD_DIM
                        o_ref[0, 0, orows, c0:c0 + HEAD_DIM] = (
                            acc_scr[hh, pl.ds(si * sub + rr, sub // r, stride=r), :].astype(BF16))

    for gi, (o_ref, r) in enumerate(zip((o0_ref, o1_ref, o2_ref), DILATIONS)):
        for normed in (True, False):
            is_kind = (kind < 2) if normed else (kind == 2)
            pl.when(jnp.logical_and(group == gi, is_kind))(
                functools.partial(tile, o_ref, r, normed))


def _proj_qkv(h, w_qkv, gains, cosf, sinf):
    bsz, s, k = h.shape
    tm, tn = 1024, GROUP_WIDTH

    def out_spec(g, r):
        return pl.BlockSpec((1, 1, tm // r, r * tn),
                            lambda b, i, j: (jnp.clip(j - 3 * g, 0, 2), b, i, 0))

    return pl.pallas_call(
        functools.partial(_qkv_kernel, tm=tm),
        grid=(bsz, s // tm, 3 * N_GROUPS),
        in_specs=[
            pl.BlockSpec((1, tm, k), lambda b, i, j: (b, i, 0)),
            pl.BlockSpec((k, tn), lambda b, i, j: (0, (j % 3) * N_GROUPS + j // 3)),
            pl.BlockSpec((1, 1, HEAD_DIM), lambda b, i, j: (j % 3, 0, 0)),
            pl.BlockSpec((1, tm, HEAD_DIM), lambda b, i, j: (b, i, 0)),
            pl.BlockSpec((1, tm, HEAD_DIM), lambda b, i, j: (b, i, 0)),
        ],
        out_specs=[out_spec(g, r) for g, r in enumerate(DILATIONS)],
        out_shape=[jax.ShapeDtypeStruct((3, bsz, s // r, r * tn), BF16) for r in DILATIONS],
        scratch_shapes=[pltpu.VMEM((GROUP_HEADS, tm, HEAD_DIM), F32)],
        compiler_params=_params(("parallel", "parallel", "arbitrary")),
        name="proj_qkv",
    )(h, w_qkv, gains, cosf, sinf)


def _mm_kernel(a_ref, *refs, mode):
    o_ref = refs[-1]
    a = a_ref[0]
    if mode == "plain":
        o_ref[0] = _dot(a, refs[0][...]).astype(o_ref.dtype)
    elif mode == "silu":
        o_ref[0] = _silu(_dot(a, refs[0][...])).astype(o_ref.dtype)
    elif mode == "glu":
        o_ref[0] = (_dot(a, refs[0][...]) * jax.nn.sigmoid(_dot(a, refs[1][...]))).astype(o_ref.dtype)
    elif mode == "swiglu":
        o_ref[0] = (_silu(_dot(a, refs[0][...])) * _dot(a, refs[1][...])).astype(o_ref.dtype)
    elif mode == "residual":
        x_ref, gate_ref = refs[1], refs[2]
        o_ref[0] = x_ref[0] + gate_ref[0] * _dot(a, refs[0][...])
    else:
        raise ValueError(mode)


def _mm(a, w, *, mode, out_dtype, tn, tm=1024, col1=0, n_out=None, x=None, gate=None):
    bsz, s, k = a.shape
    n_out = w.shape[1] if n_out is None else n_out
    assert col1 % tn == 0 and n_out % tn == 0 and s % tm == 0
    j1 = col1 // tn
    in_specs = [pl.BlockSpec((1, tm, k), lambda b, i, j: (b, i, 0)),
                pl.BlockSpec((k, tn), lambda b, i, j: (0, j))]
    args = [a, w]
    if mode in ("glu", "swiglu"):
        in_specs.append(pl.BlockSpec((k, tn), lambda b, i, j: (0, j1 + j)))
        args.append(w)
    if mode == "residual":
        in_specs += [pl.BlockSpec((1, tm, tn), lambda b, i, j: (b, i, j)),
                     pl.BlockSpec((1, 1, tn), lambda b, i, j: (b, 0, j))]
        args += [x, gate]
    return pl.pallas_call(
        functools.partial(_mm_kernel, mode=mode),
        grid=(bsz, s // tm, n_out // tn),
        in_specs=in_specs,
        out_specs=pl.BlockSpec((1, tm, tn), lambda b, i, j: (b, i, j)),
        out_shape=jax.ShapeDtypeStruct((bsz, s, n_out), out_dtype),
        compiler_params=_params(("parallel", "parallel", "arbitrary")),
        name="mm_" + mode,
    )(*args)


def _gdn_proj_kernel(a_ref, w_ref, cw_ref, o_ref, slab, carry, *, tm):
    i = pl.program_id(1)
    j = pl.program_id(2)
    pad = SUBLANES
    sub = tm // QKV_SUBTILES
    heads_per_tile = GDN_PROJ_TN // HEAD_DIM

    @pl.when(i == 0)
    def _():
        slab[0:pad, :] = jnp.zeros((pad, GDN_PROJ_TN), F32)

    @pl.when(i > 0)
    def _():
        slab[0:pad, :] = carry[j]

    def project(si):
        r0 = si * sub
        acc = _dot(a_ref[0, r0:r0 + sub, :], w_ref[...])
        slab[pad + r0:pad + r0 + sub, :] = acc
        return acc

    def tile(normed, out_scale):
        nxt = project(0)
        for si in range(QKV_SUBTILES):
            r0 = si * sub
            acc = nxt
            if si + 1 < QKV_SUBTILES:
                nxt = project(si + 1)
            for hh in range(heads_per_tile):
                cols = slice(hh * HEAD_DIM, (hh + 1) * HEAD_DIM)
                y = cw_ref[SHORT_CONV - 1:SHORT_CONV, cols] * acc[:, cols]
                for kk in reversed(range(SHORT_CONV - 1)):
                    y = y + cw_ref[kk:kk + 1, cols] * slab[pl.ds(pad + r0 - SHORT_CONV + 1 + kk, sub), cols]
                y = _silu(y)
                if normed:
                    y = y * (lax.rsqrt(jnp.sum(y * y, axis=-1, keepdims=True) + NORM_EPS) * out_scale)
                o_ref[0, r0:r0 + sub, cols] = y.astype(BF16)
        carry[j] = slab[tm:tm + pad, :]

    q_tiles = GDN_WIDTH // GDN_PROJ_TN
    pl.when(j < q_tiles)(functools.partial(tile, True, HEAD_DIM ** -0.5))
    pl.when(jnp.logical_and(j >= q_tiles, j < 2 * q_tiles))(functools.partial(tile, True, 1.0))
    pl.when(j >= 2 * q_tiles)(functools.partial(tile, False, 1.0))


def _gdn_proj(h, w_d, conv_w):
    bsz, s, k = h.shape
    n = w_d.shape[1]
    tm, tn = 1024, GDN_PROJ_TN
    return pl.pallas_call(
        functools.partial(_gdn_proj_kernel, tm=tm),
        grid=(bsz, s // tm, n // tn),
        in_specs=[pl.BlockSpec((1, tm, k), lambda b, i, j: (b, i, 0)),
                  pl.BlockSpec((k, tn), lambda b, i, j: (0, j)),
                  pl.BlockSpec((SHORT_CONV, tn), lambda b, i, j: (0, j))],
        out_specs=pl.BlockSpec((1, tm, tn), lambda b, i, j: (b, i, j)),
        out_shape=jax.ShapeDtypeStruct((bsz, s, n), BF16),
        scratch_shapes=[pltpu.VMEM((tm + SUBLANES, tn), F32),
                        pltpu.VMEM((n // tn, SUBLANES, tn), F32)],
        compiler_params=_params(("parallel", "arbitrary", "arbitrary")),
        name="gdn_proj",
    )(h, w_d, conv_w)


def _attn_kernel(q_ref, kc_ref, kp_ref, vc_ref, vp_ref, o_ref, l_ref, o_scr, *, r, nq):
    n = pl.program_id(1)
    qi = lax.broadcasted_iota(jnp.int32, (ATTN_BLOCK, ATTN_BLOCK), 0)
    kj = lax.broadcasted_iota(jnp.int32, (ATTN_BLOCK, ATTN_BLOCK), 1)
    cur_ok = kj <= qi
    prev_band = kj >= qi
    prev_first = jnp.logical_and(prev_band, n > 0)
    for rr in range(r):
        for t in range(nq):
            rows = slice(t * ATTN_BLOCK, (t + 1) * ATTN_BLOCK)
            prows = slice((t - 1) * ATTN_BLOCK, t * ATTN_BLOCK)
            prev_ok = prev_first if t == 0 else prev_band
            heads = range(GROUP_HEADS)
            cols = [slice(rr * GROUP_WIDTH + hh * HEAD_DIM, rr * GROUP_WIDTH + (hh + 1) * HEAD_DIM)
                    for hh in heads]
            if t == 0:
                kps = [kp_ref[0, 0, :, cols[hh]] for hh in heads]
                vps = [vp_ref[0, 0, :, cols[hh]] for hh in heads]
            else:
                kps = [kc_ref[0, 0, prows, cols[hh]] for hh in heads]
                vps = [vc_ref[0, 0, prows, cols[hh]] for hh in heads]
            qs = [q_ref[0, 0, rows, cols[hh]] for hh in heads]
            s_c = [jnp.where(cur_ok, _dot_nt(qs[hh], kc_ref[0, 0, rows, cols[hh]]), NEG_BIG) for hh in heads]
            s_p = [jnp.where(prev_ok, _dot_nt(qs[hh], kps[hh]), NEG_BIG) for hh in heads]
            m = [jnp.maximum(jnp.max(s_c[hh], axis=-1, keepdims=True),
                             jnp.max(s_p[hh], axis=-1, keepdims=True)) for hh in heads]
            p_c = [jnp.exp(s_c[hh] - m[hh]) for hh in heads]
            p_p = [jnp.exp(s_p[hh] - m[hh]) for hh in heads]
            den = [jnp.sum(p_c[hh], axis=-1, keepdims=True) + jnp.sum(p_p[hh], axis=-1, keepdims=True)
                   for hh in heads]
            acc = [_dot(p_c[hh].astype(BF16), vc_ref[0, 0, rows, cols[hh]])
                   + _dot(p_p[hh].astype(BF16), vps[hh]) for hh in heads]
            for hh in heads:
                o = acc[hh] / den[hh]
                if r == 1:
                    o_scr[hh, rows, :] = o
                else:
                    o_scr[hh, pl.ds(t * ATTN_BLOCK * r + rr, ATTN_BLOCK, stride=r), :] = o
                lcol = rr * GROUP_HEADS + hh
                l_ref[0, rows, lcol:lcol + 1] = m[hh] + jnp.log(den[hh])
    for hh in range(GROUP_HEADS):
        o_ref[0, :, hh * HEAD_DIM:(hh + 1) * HEAD_DIM] = o_scr[hh].astype(BF16)


def _attn_group(qkv, g):
    _, bsz, l, rw = qkv.shape
    r, nq = DILATIONS[g], ATTN_NQ[g]
    tq = ATTN_BLOCK * nq
    tp = tq * r
    s = l * r

    def cur(kind):
        return pl.BlockSpec((1, 1, tq, rw), lambda b, n: (kind, b, n, 0))

    def prev(kind):
        return pl.BlockSpec((1, 1, ATTN_BLOCK, rw), lambda b, n: (kind, b, jnp.maximum(n * nq - 1, 0), 0))

    o, lse = pl.pallas_call(
        functools.partial(_attn_kernel, r=r, nq=nq),
        grid=(bsz, l // tq),
        in_specs=[cur(0), cur(1), prev(1), cur(2), prev(2)],
        out_specs=[pl.BlockSpec((1, tp, GROUP_WIDTH), lambda b, n: (b, n, 0)),
                   pl.BlockSpec((1, tq, r * GROUP_HEADS), lambda b, n: (b, n, 0))],
        out_shape=[jax.ShapeDtypeStruct((bsz, s, GROUP_WIDTH), BF16),
                   jax.ShapeDtypeStruct((bsz, l, r * GROUP_HEADS), F32)],
        scratch_shapes=[pltpu.VMEM((GROUP_HEADS, tp, HEAD_DIM), F32)],
        compiler_params=_params(("parallel", "arbitrary")),
        name=f"attn_g{g}",
    )(qkv, qkv, qkv, qkv, qkv)
    return o, lse.reshape(bsz, s, GROUP_HEADS)


def _attn_merge_kernel(o0, o1, o2, l0, l1, l2, out_ref):
    a0, a1, a2 = l0[0], l1[0], l2[0]
    m = jnp.maximum(jnp.maximum(a0, a1), a2)
    e0, e1, e2 = jnp.exp(a0 - m), jnp.exp(a1 - m), jnp.exp(a2 - m)
    inv = 1.0 / (e0 + e1 + e2)
    w0, w1, w2 = e0 * inv, e1 * inv, e2 * inv
    for hh in range(GROUP_HEADS):
        cols = slice(hh * HEAD_DIM, (hh + 1) * HEAD_DIM)
        hc = slice(hh, hh + 1)
        acc = w0[:, hc] * o0[0, :, cols].astype(F32)
        acc = acc + w1[:, hc] * o1[0, :, cols].astype(F32)
        acc = acc + w2[:, hc] * o2[0, :, cols].astype(F32)
        out_ref[0, :, cols] = acc.astype(BF16)


def _attn_merge(outs, lses):
    bsz, s, w = outs[0].shape
    tm = 1024
    blk = pl.BlockSpec((1, tm, w), lambda b, i: (b, i, 0))
    lblk = pl.BlockSpec((1, tm, GROUP_HEADS), lambda b, i: (b, i, 0))
    return pl.pallas_call(
        _attn_merge_kernel,
        grid=(bsz, s // tm),
        in_specs=[blk] * 3 + [lblk] * 3,
        out_specs=blk,
        out_shape=jax.ShapeDtypeStruct((bsz, s, w), BF16),
        compiler_params=_params(("parallel", "parallel")),
        name="attn_merge",
    )(*outs, *lses)


def _conformer_kernel(h_ref, wu_ref, w_ref, b_ref, g_ref, beta_ref, o_ref, scr, sh_scr, y_scr, *, tm):
    i = pl.program_id(1)
    pad = 32

    @pl.when(i == 0)
    def _():
        scr[0:pad, :] = jnp.zeros((pad, CONV_CH), F32)

    @pl.when(i > 0)
    def _():
        scr[0:pad, :] = scr[tm:tm + pad, :]

    h = h_ref[0]
    scr[pad:pad + tm, :] = (_dot(h, wu_ref[:, :CONV_CH])
                            * jax.nn.sigmoid(_dot(h, wu_ref[:, CONV_CH:])))
    for b in range(1, SUBLANES):
        sh_scr[b - 1] = scr[pl.ds(b, tm + pad - SUBLANES), :]
    for c in range(CONV_CH // HEAD_DIM):
        cols = slice(c * HEAD_DIM, (c + 1) * HEAD_DIM)
        acc = None
        for kk in range(CONV_K):
            a8, b = divmod(pad - CONV_K + 1 + kk, SUBLANES)
            src = scr if b == 0 else sh_scr.at[b - 1]
            term = w_ref[kk:kk + 1, cols] * src[pl.ds(a8 * SUBLANES, tm), cols]
            acc = term if acc is None else acc + term
        y_scr[:, cols] = acc + b_ref[:, cols]
    y = y_scr[...]
    mu = jnp.mean(y, axis=-1, keepdims=True)
    yc = y - mu
    var = jnp.mean(yc * yc, axis=-1, keepdims=True)
    t = yc * lax.rsqrt(var + NORM_EPS) * g_ref[...] + beta_ref[...]
    o_ref[0] = _silu(t).astype(BF16)


def _conformer(h, w_u, w, b, g, beta):
    bsz, s, d = h.shape
    n = CONV_CH
    tm = 256
    row = pl.BlockSpec((1, n), lambda bb, i: (0, 0))
    return pl.pallas_call(
        functools.partial(_conformer_kernel, tm=tm),
        grid=(bsz, s // tm),
        in_specs=[
            pl.BlockSpec((1, tm, d), lambda bb, i: (bb, i, 0)),
            pl.BlockSpec((d, 2 * n), lambda bb, i: (0, 0)),
            pl.BlockSpec((32, n), lambda bb, i: (0, 0)),
            row, row, row,
        ],
        out_specs=pl.BlockSpec((1, tm, n), lambda bb, i: (bb, i, 0)),
        out_shape=jax.ShapeDtypeStruct((bsz, s, n), BF16),
        scratch_shapes=[pltpu.VMEM((tm + 32, n), F32),
                        pltpu.VMEM((SUBLANES - 1, tm + 32 - SUBLANES, n), F32),
                        pltpu.VMEM((tm, n), F32)],
        compiler_params=_params(("parallel", "arbitrary")),
        name="conformer_conv",
    )(h, w_u, w, b, g, beta)


def _gdn_scalar_kernel(h_ref, w_ref, alog_ref, dtb_ref, o_ref, *, ts):
    x = _dot_nt(w_ref[...], h_ref[0])
    beta = jax.nn.sigmoid(x[0:GDN_HEADS])
    g = -jnp.exp(alog_ref[...]) * jax.nn.softplus(x[GDN_HEADS:2 * GDN_HEADS] + dtb_ref[...])
    lane = lax.broadcasted_iota(jnp.int32, (GDN_HEADS, ts), 1) % GDN_CHUNK
    fwd, bwd = g, g
    sh = 1
    while sh < GDN_CHUNK:
        fwd = fwd + jnp.where(lane >= sh, pltpu.roll(fwd, sh, 1), 0.0)
        bwd = bwd + jnp.where(lane < GDN_CHUNK - sh, pltpu.roll(bwd, ts - sh, 1), 0.0)
        sh *= 2
    rest = bwd - g
    eg = jnp.exp(fwd)
    o_ref[0, 0 * GDN_HEADS:1 * GDN_HEADS] = beta
    o_ref[0, 1 * GDN_HEADS:2 * GDN_HEADS] = eg
    o_ref[0, 2 * GDN_HEADS:3 * GDN_HEADS] = beta * eg
    o_ref[0, 3 * GDN_HEADS:4 * GDN_HEADS] = jnp.exp(rest)
    o_ref[0, 4 * GDN_HEADS:5 * GDN_HEADS] = fwd
    o_ref[0, 5 * GDN_HEADS:6 * GDN_HEADS] = jnp.exp(fwd + rest)


def _gdn_scalars(h, w_ba_t, a_log, dt_bias):
    bsz, s, d = h.shape
    ts = 2048
    col = pl.BlockSpec((GDN_HEADS, 1), lambda b, i: (0, 0))
    return pl.pallas_call(
        functools.partial(_gdn_scalar_kernel, ts=ts),
        grid=(bsz, s // ts),
        in_specs=[pl.BlockSpec((1, ts, d), lambda b, i: (b, i, 0)),
                  pl.BlockSpec((2 * GDN_HEADS, d), lambda b, i: (0, 0)), col, col],
        out_specs=pl.BlockSpec((1, 6 * GDN_HEADS, ts), lambda b, i: (b, 0, i)),
        out_shape=jax.ShapeDtypeStruct((bsz, 6 * GDN_HEADS, s), F32),
        compiler_params=_params(("parallel", "parallel")),
        name="gdn_scalars",
    )(h, w_ba_t, a_log.reshape(GDN_HEADS, 1), dt_bias.reshape(GDN_HEADS, 1))


def _gdn_kernel(x_ref, z_ref, col_ref, grow_ref, dc_ref, ng_ref, o_ref, state):
    i = pl.program_id(1)
    ts = GDN_TILE

    @pl.when(i == 0)
    def _():
        state[...] = jnp.zeros_like(state)

    ri = lax.broadcasted_iota(jnp.int32, (ts, ts), 0)
    ci = lax.broadcasted_iota(jnp.int32, (ts, ts), 1)
    causal = ri >= ci
    strict = ri > ci

    def head_cols(part, h):
        c0 = part * GDN_WIDTH + h * HEAD_DIM
        return slice(c0, c0 + HEAD_DIM)

    for h0 in range(0, GDN_HEADS, GDN_LOCKSTEP):
        heads = range(h0, h0 + GDN_LOCKSTEP)
        pre = {}
        for h in heads:
            q = x_ref[0, :, head_cols(0, h)].astype(F32)
            k = x_ref[0, :, head_cols(1, h)].astype(F32)
            v = x_ref[0, :, head_cols(2, h)].astype(F32)
            beta, gam, bgam, kdec, gcol = (col_ref[0, :, c * GDN_HEADS + h:c * GDN_HEADS + h + 1]
                                           for c in range(5))
            grow = grow_ref[0, h:h + 1, :]
            ktb = k.T.astype(BF16)
            kb = k.astype(BF16)
            decay = jnp.exp(jnp.where(causal, gcol - grow, NEG_BIG))
            a = jnp.where(strict, beta * _dot(kb, ktb) * decay, 0.0)
            qkb = (_dot(q.astype(BF16), ktb) * decay).astype(BF16)
            sol = jnp.concatenate([v * beta, k * bgam], axis=1)
            pre[h] = dict(p=a.astype(BF16), sol=sol, qkb=qkb,
                          qd=(q * gam).astype(BF16), kdt=(k * kdec).T.astype(BF16))

        for h in heads:
            d = pre[h]
            d["sol"] = d["sol"] - _dot(d["p"], d["sol"].astype(BF16))
        for _ in range(GDN_NEUMANN_SQUARINGS):
            for h in heads:
                d = pre[h]
                d["p"] = _dot(d["p"], d["p"]).astype(BF16)
            for h in heads:
                d = pre[h]
                d["sol"] = d["sol"] + _dot(d["p"], d["sol"].astype(BF16))

        for h in heads:
            d = pre[h]
            hcols = slice(h * HEAD_DIM, (h + 1) * HEAD_DIM)
            st = state[h]
            sb = st.astype(BF16)
            v_new = d["sol"][:, :HEAD_DIM] - _dot(d["sol"][:, HEAD_DIM:].astype(BF16), sb)
            vb = v_new.astype(BF16)
            o = _dot(d["qd"], sb) + _dot(d["qkb"], vb)
            state[h] = st * dc_ref[0, 0, :, hcols] + _dot(d["kdt"], vb)
            o = o * lax.rsqrt(jnp.mean(o * o, axis=-1, keepdims=True) + NORM_EPS) * ng_ref[...]
            o_ref[0, :, hcols] = (o * z_ref[0, :, hcols].astype(F32)).astype(BF16)


def _gdn(qkv_d, z, cols, grow, dcb, norm_g):
    bsz, s, n = qkv_d.shape
    ts = GDN_TILE
    return pl.pallas_call(
        _gdn_kernel,
        grid=(bsz, s // ts),
        in_specs=[
            pl.BlockSpec((1, ts, n), lambda b, i: (b, i, 0)),
            pl.BlockSpec((1, ts, GDN_WIDTH), lambda b, i: (b, i, 0)),
            pl.BlockSpec((1, ts, 5 * GDN_HEADS), lambda b, i: (b, i, 0)),
            pl.BlockSpec((1, GDN_HEADS, ts), lambda b, i: (b, 0, i)),
            pl.BlockSpec((1, 1, ts // GDN_CHUNK, GDN_WIDTH), lambda b, i: (b, i, 0, 0)),
            pl.BlockSpec((1, HEAD_DIM), lambda b, i: (0, 0)),
        ],
        out_specs=pl.BlockSpec((1, ts, GDN_WIDTH), lambda b, i: (b, i, 0)),
        out_shape=jax.ShapeDtypeStruct((bsz, s, GDN_WIDTH), BF16),
        scratch_shapes=[pltpu.VMEM((GDN_HEADS, HEAD_DIM, HEAD_DIM), F32)],
        compiler_params=_params(("parallel", "arbitrary")),
        name="gdn_delta",
    )(qkv_d, z, cols, grow, dcb, norm_g)


def _merge_kernel(oa_ref, od_ref, oc_ref, h_ref, wa_ref, wd_ref, wc_ref, ga_ref, gb_ref, gc_ref, o_ref):
    h = h_ref[0]
    acc = jax.nn.sigmoid(_dot(h, ga_ref[...])) * _dot(oa_ref[0], wa_ref[...])
    acc = acc + jax.nn.sigmoid(_dot(h, gb_ref[...])) * _dot(od_ref[0], wd_ref[...])
    acc = acc + jax.nn.sigmoid(_dot(h, gc_ref[...])) * _dot(oc_ref[0], wc_ref[...])
    o_ref[0] = acc.astype(BF16)


def _merge(o_a, o_d, o_c, h, w_a, w_d, w_c, w_gate):
    bsz, s, d = h.shape
    tm, tn = 1024, 256
    nd = d // tn

    def rows(width):
        return pl.BlockSpec((1, tm, width), lambda b, i, j: (b, i, 0))

    def wcols(kdim, off=0):
        return pl.BlockSpec((kdim, tn), lambda b, i, j: (0, off + j))

    return pl.pallas_call(
        _merge_kernel,
        grid=(bsz, s // tm, nd),
        in_specs=[rows(o_a.shape[2]), rows(o_d.shape[2]), rows(o_c.shape[2]), rows(d),
                  wcols(w_a.shape[0]), wcols(w_d.shape[0]), wcols(w_c.shape[0]),
                  wcols(d), wcols(d, nd), wcols(d, 2 * nd)],
        out_specs=pl.BlockSpec((1, tm, tn), lambda b, i, j: (b, i, j)),
        out_shape=jax.ShapeDtypeStruct((bsz, s, d), BF16),
        compiler_params=_params(("parallel", "parallel", "arbitrary")),
        name="branch_merge",
    )(o_a, o_d, o_c, h, w_a, w_d, w_c, w_gate, w_gate, w_gate)


def _out_norm_kernel(m_ref, w_ref, x_ref, gate_ref, g_ref, sc_ref, sh_ref, x1_ref, h_ref, *, tm):
    sub = tm // OUT_SUBTILES
    for si in range(OUT_SUBTILES):
        rows = slice(si * sub, (si + 1) * sub)
        x1 = x_ref[0, rows, :] + gate_ref[0] * _dot(m_ref[0, rows, :], w_ref[...])
        x1_ref[0, rows, :] = x1
        ms = jnp.mean(x1 * x1, axis=-1, keepdims=True)
        y = x1 * lax.rsqrt(ms + NORM_EPS) * g_ref[...]
        h_ref[0, rows, :] = (y * (1.0 + sc_ref[0]) + sh_ref[0]).astype(BF16)


def _out_norm(m, w, x, gate, g, scale, shift):
    bsz, s, d = x.shape
    tm = 512
    vec = pl.BlockSpec((1, 1, d), lambda b, i: (b, 0, 0))
    blk = pl.BlockSpec((1, tm, d), lambda b, i: (b, i, 0))
    return pl.pallas_call(
        functools.partial(_out_norm_kernel, tm=tm),
        grid=(bsz, s // tm),
        in_specs=[blk, pl.BlockSpec((d, d), lambda b, i: (0, 0)), blk, vec,
                  pl.BlockSpec((1, d), lambda b, i: (0, 0)), vec, vec],
        out_specs=[blk, blk],
        out_shape=[jax.ShapeDtypeStruct((bsz, s, d), F32), jax.ShapeDtypeStruct((bsz, s, d), BF16)],
        compiler_params=_params(("parallel", "parallel")),
        name="out_proj_norm",
    )(m, w, x, gate, g, scale, shift)


_C_GDN = 3 * ATTN_WIDTH
_C_BA = _C_GDN + 3 * GDN_WIDTH
_C_Z = _C_BA + 2 * GDN_HEADS
_C_U = _C_Z + GDN_WIDTH
_C_GATE = _C_U + 2 * CONV_CH
_C_END = _C_GATE + 3 * D_MODEL


def _mixer(h, l, cosf, sinf, w_in, q_norm_g, k_norm_g, w_attn_o, gdn_conv_w, gdn_a_log, gdn_dt_bias,
           gdn_norm_g, w_gdn_o, conv_dw_w, conv_dw_b, conv_ln_g, conv_ln_b, w_conv_o):
    bsz, s, _ = h.shape
    w = w_in[l]
    w_qkv = w[:, :_C_GDN].astype(BF16)
    w_d = w[:, _C_GDN:_C_BA].astype(BF16)
    w_ba_t = w[:, _C_BA:_C_Z].T.astype(BF16)
    w_z = w[:, _C_Z:_C_U].astype(BF16)
    w_u = w[:, _C_U:_C_GATE].astype(BF16)
    w_gate = w[:, _C_GATE:_C_END].astype(BF16)

    gains = jnp.stack([q_norm_g[l] * (HEAD_DIM ** -0.5), k_norm_g[l],
                       jnp.ones((HEAD_DIM,), F32)]).reshape(3, 1, HEAD_DIM)
    qkv_groups = _proj_qkv(h, w_qkv, gains, cosf, sinf)
    outs, lses = zip(*[_attn_group(qkv_groups[g], g) for g in range(N_GROUPS)])
    o_a = _attn_merge(outs, lses)

    qkv_d = _gdn_proj(h, w_d, gdn_conv_w[l])
    z = _mm(h, w_z, mode="silu", out_dtype=BF16, tn=512)
    sc = _gdn_scalars(h, w_ba_t, gdn_a_log[l], gdn_dt_bias[l])
    cols = jnp.swapaxes(sc[:, :5 * GDN_HEADS], 1, 2)
    grow = sc[:, 4 * GDN_HEADS:5 * GDN_HEADS]
    dcb = jnp.swapaxes(sc[:, 5 * GDN_HEADS:, ::GDN_CHUNK], 1, 2)
    dcb = jnp.broadcast_to(dcb[..., None], dcb.shape + (HEAD_DIM,))
    dcb = dcb.reshape(bsz, s // GDN_TILE, GDN_TILE // GDN_CHUNK, GDN_WIDTH)
    o_d = _gdn(qkv_d, z, cols, grow, dcb, gdn_norm_g[l].reshape(1, HEAD_DIM))

    dw = jnp.concatenate([conv_dw_w[l], jnp.zeros((1, CONV_CH), F32)], axis=0)
    o_c = _conformer(h, w_u, dw, conv_dw_b[l].reshape(1, CONV_CH), conv_ln_g[l].reshape(1, CONV_CH),
                     conv_ln_b[l].reshape(1, CONV_CH))

    return _merge(o_a, o_d, o_c, h, w_attn_o[l].astype(BF16), w_gdn_o[l].astype(BF16),
                  w_conv_o[l].astype(BF16), w_gate)


def kernel(x, c, positions, mix_mod_w, mix_mod_b, mix_norm_g, w_in, q_norm_g, k_norm_g, w_attn_o, gdn_conv_w, gdn_a_log, gdn_dt_bias, gdn_norm_g, w_gdn_o, conv_dw_w, conv_dw_b, conv_ln_g, conv_ln_b, w_conv_o, w_out, ffn_mod_w, ffn_mod_b, ffn_norm_g, w_gate_up, w_down):
    bsz, s, d = x.shape
    depth = w_in.shape[0]
    cosf, sinf = _rope_tables(positions)
    c_pad = jnp.concatenate([c, jnp.zeros((SUBLANES - bsz, d), c.dtype)], axis=0)
    mix_mods = _adaln_mods(c_pad, mix_mod_w, mix_mod_b)
    ffn_mods = _adaln_mods(c_pad, ffn_mod_w, ffn_mod_b)

    def mods(m, l):
        return tuple(m[l, :bsz, None, k * d:(k + 1) * d] for k in range(3))

    for l in range(depth):
        shift, scale, gate = mods(mix_mods, l)
        h = _mod_norm(x, mix_norm_g[l].reshape(1, d), scale, shift)
        m = _mixer(h, l, cosf, sinf, w_in, q_norm_g, k_norm_g, w_attn_o, gdn_conv_w, gdn_a_log,
                   gdn_dt_bias, gdn_norm_g, w_gdn_o, conv_dw_w, conv_dw_b, conv_ln_g, conv_ln_b, w_conv_o)
        shift2, scale2, gate2 = mods(ffn_mods, l)
        x, h = _out_norm(m, w_out[l].astype(BF16), x, gate, ffn_norm_g[l].reshape(1, d), scale2, shift2)
        gate = gate2
        wgu = w_gate_up[l].astype(BF16)
        act = _mm(h, wgu, mode="swiglu", out_dtype=BF16, tn=512, col1=D_FF, n_out=D_FF)
        x = _mm(act, w_down[l].astype(BF16), mode="residual", out_dtype=F32, tn=512, x=x, gate=gate)
    return x
```
